```python
import jax, jax.numpy as jnp
from jax import lax
import numpy as np

D_MODEL = 1024
BATCH = 8
SEQ = 2048
DEPTH = 1
DEC_BATCH = 32
DEC_SEQ = 32
PAST_LEN = 4096

CHUNK = 64
D_CONV = D_MODEL
D_POOL = D_MODEL
CONV_W = 3
POOL_WINDOWS = (2, 4, 8, 16)
N_POOL_GROUPS = len(POOL_WINDOWS)
POOL_GW = D_POOL // N_POOL_GROUPS
POOL_HIST = max(POOL_WINDOWS) - 1
PLE_DIM = 256
N_IN_COLS = 4 * D_CONV + 2 * D_POOL + 2 * D_MODEL
EPS = 1e-6

kernel_name = "gated_conv_pool_streaming_encoder_step"


def _rmsnorm(x, g):
    xf = x.astype(jnp.float32)
    y = xf * lax.rsqrt(jnp.mean(xf * xf, axis=-1, keepdims=True) + EPS)
    return (y * g.astype(jnp.float32)).astype(x.dtype)


def _layer(x, p, conv_hist, pool_hist, offset, g_pre, w_in, conv_w, conv_b, w_grp,
           pool_scale, w_a_out, w_b_out, w_o, g_post, w_ple, w_pgate):
    b, L, _ = x.shape
    h = _rmsnorm(x, g_pre)
    proj = h @ w_in
    sizes = [D_CONV, D_CONV, D_CONV, D_CONV, D_POOL, D_POOL, D_MODEL, D_MODEL]
    cuts = list(np.cumsum(sizes)[:-1])
    xa, ba, ca, za, xb, zb, ga, gb = jnp.split(proj, cuts, axis=-1)

    u = ca * xa
    up = jnp.concatenate([conv_hist.astype(u.dtype), u], axis=1)
    conv = conv_b + sum(conv_w[k] * up[:, k:k + L] for k in range(CONV_W))
    ya = ((ba * conv) * jax.nn.silu(za)) @ w_a_out
    new_conv = up[:, -(CONV_W - 1):]

    bp = jnp.concatenate([pool_hist.astype(xb.dtype), xb], axis=1)
    cs = jnp.cumsum(bp.astype(jnp.float32), axis=1)
    cs0 = jnp.concatenate([jnp.zeros((b, 1, D_POOL), jnp.float32), cs], axis=1)
    pos = offset + jnp.arange(L)
    start = POOL_HIST + 1
    pooled = []
    for g, w in enumerate(POOL_WINDOWS):
        sl = slice(g * POOL_GW, (g + 1) * POOL_GW)
        s = cs0[:, start:, sl] - cs0[:, start - w:start - w + L, sl]
        cnt = jnp.minimum(pos + 1, w).astype(jnp.float32)
        pooled.append(s / cnt[None, :, None])
    pooled = jnp.concatenate(pooled, axis=-1)
    d = (pooled - xb.astype(jnp.float32)).astype(xb.dtype).reshape(b, L, N_POOL_GROUPS, POOL_GW)
    mixed = jnp.einsum('blgc,gcd->blgd', d, w_grp).reshape(b, L, D_POOL) * pool_scale
    yb = (mixed * jax.nn.silu(zb)) @ w_b_out
    new_pool = bp[:, -POOL_HIST:]

    m = jax.nn.sigmoid(ga) * ya + jax.nn.sigmoid(gb) * yb
    o = m @ w_o
    x1 = x + _rmsnorm(o, g_post)

    e = (p @ w_ple) * jax.nn.sigmoid(x1 @ w_pgate)
    return x1 + e, new_conv, new_pool


def setup_inputs(seed: int = 0) -> dict:
    key = jax.random.key(seed)
    ks = jax.random.split(key, 20)
    f32 = jnp.float32
    nrm = lambda k, shape, scale: jax.random.normal(k, shape, f32) * scale
    return {
        "x_prompt": nrm(ks[0], (BATCH, SEQ, D_MODEL), 1.0),
        "x_sample": nrm(ks[1], (DEC_BATCH, DEC_SEQ, D_MODEL), 1.0),
        "p_prompt": nrm(ks[2], (DEPTH, BATCH, SEQ, PLE_DIM), 1.0),
        "p_sample": nrm(ks[3], (DEPTH, DEC_BATCH, DEC_SEQ, PLE_DIM), 1.0),
        "cache_conv": nrm(ks[4], (DEPTH, DEC_BATCH, CONV_W - 1, D_CONV), 1.0),
        "state_pool": nrm(ks[5], (DEPTH, DEC_BATCH, POOL_HIST, D_POOL), 1.0),
        "g_pre": 1.0 + nrm(ks[6], (DEPTH, D_MODEL), 0.05),
        "w_in": nrm(ks[7], (DEPTH, D_MODEL, N_IN_COLS), D_MODEL ** -0.5),
        "conv_w": nrm(ks[8], (DEPTH, CONV_W, D_CONV), CONV_W ** -0.5),
        "conv_b": nrm(ks[9], (DEPTH, D_CONV), 0.02),
        "w_grp": nrm(ks[10], (DEPTH, N_POOL_GROUPS, POOL_GW, POOL_GW), POOL_GW ** -0.5),
        "pool_scale": 1.0 + nrm(ks[11], (DEPTH, D_POOL), 0.05),
        "w_a_out": nrm(ks[12], (DEPTH, D_CONV, D_MODEL), D_CONV ** -0.5),
        "w_b_out": nrm(ks[13], (DEPTH, D_POOL, D_MODEL), D_POOL ** -0.5),
        "w_o": nrm(ks[14], (DEPTH, D_MODEL, D_MODEL), D_MODEL ** -0.5),
        "g_post": 1.0 + nrm(ks[15], (DEPTH, D_MODEL), 0.05),
        "w_ple": nrm(ks[16], (DEPTH, PLE_DIM, D_MODEL), PLE_DIM ** -0.5),
        "w_pgate": nrm(ks[17], (DEPTH, D_MODEL, D_MODEL), D_MODEL ** -0.5),
    }


def reference(x_prompt, x_sample, p_prompt, p_sample, cache_conv, state_pool,
              g_pre, w_in, conv_w, conv_b, w_grp, pool_scale, w_a_out, w_b_out,
              w_o, g_post, w_ple, w_pgate):
    hp = x_prompt
    hs = x_sample
    conv_p, conv_s, pool_p, pool_s = [], [], [], []
    bp = x_prompt.shape[0]
    for i in range(DEPTH):
        lw = (g_pre[i], w_in[i], conv_w[i], conv_b[i], w_grp[i], pool_scale[i],
              w_a_out[i], w_b_out[i], w_o[i], g_post[i], w_ple[i], w_pgate[i])
        zc = jnp.zeros((bp, CONV_W - 1, D_CONV), hp.dtype)
        zp = jnp.zeros((bp, POOL_HIST, D_POOL), hp.dtype)
        hp, nc_p, np_p = _layer(hp, p_prompt[i], zc, zp, 0, *lw)
        hs, nc_s, np_s = _layer(hs, p_sample[i], cache_conv[i], state_pool[i], PAST_LEN, *lw)
        conv_p.append(nc_p); pool_p.append(np_p)
        conv_s.append(nc_s); pool_s.append(np_s)
    new_conv_prompt = jnp.stack(conv_p)
    new_conv_sample = jnp.stack(conv_s)
    new_pool_prompt = jnp.stack(pool_p)
    new_pool_sample = jnp.stack(pool_s)
    return (hp, hs, new_conv_prompt, new_conv_sample, new_pool_prompt, new_pool_sample)
```

```python
import functools

import jax
import jax.numpy as jnp
from jax import lax
from jax.experimental import pallas as pl
from jax.experimental.pallas import tpu as pltpu

D_MODEL = 1024
CONV_W = 3
POOL_WINDOWS = (2, 4, 8, 16)
N_POOL_GROUPS = len(POOL_WINDOWS)
POOL_GW = D_MODEL // N_POOL_GROUPS
POOL_HIST = max(POOL_WINDOWS) - 1
PLE_DIM = 256
PAST_LEN = 4096
EPS = 1e-6

SEG_XA, SEG_BA, SEG_CA, SEG_ZA, SEG_XB, SEG_ZB, SEG_GA, SEG_GB = range(8)

CONV_PAD = 8
POOL_PAD = 16
assert CONV_PAD >= CONV_W - 1 and POOL_PAD >= POOL_HIST

V7X_VMEM_LIMIT_BYTES = 56 * 1024 * 1024

PROMPT_TILE = 256
SAMPLE_STREAMS = 8


def _rmsnorm(x, g):
    ms = jnp.mean(x * x, axis=-1, keepdims=True)
    return x * lax.rsqrt(ms + EPS) * g


def _layer_kernel(x_ref, p_ref, ch_ref, ph_ref, g_pre_ref, w_in_ref, conv_w_ref,
                  conv_b_ref, w_grp_ref, pool_scale_ref, w_a_ref, w_b_ref, w_o_ref,
                  g_post_ref, w_ple_ref, w_pgate_ref,
                  y_ref, nc_ref, np_ref, ubuf, xbuf, *, past_len):
    S, L, D = x_ref.shape
    M = S * L
    f32, bf16 = jnp.float32, jnp.bfloat16
    t = pl.program_id(1)

    @pl.when(t == 0)
    def _():
        ubuf[:, CONV_PAD - (CONV_W - 1):CONV_PAD, :] = ch_ref[...]
        xbuf[:, POOL_PAD - POOL_HIST:POOL_PAD, :] = ph_ref[...]

    x = x_ref[...].reshape(M, D)
    h = _rmsnorm(x, g_pre_ref[...]).astype(bf16)

    def proj(seg):
        return jnp.dot(h, w_in_ref[:, seg * D:(seg + 1) * D], preferred_element_type=f32)

    u = (proj(SEG_CA) * proj(SEG_XA)).reshape(S, L, D)
    ubuf[:, CONV_PAD:CONV_PAD + L, :] = u
    conv = conv_b_ref[...] + conv_w_ref[CONV_W - 1:CONV_W, :] * u
    for k in range(CONV_W - 1):
        lo = CONV_PAD - (CONV_W - 1) + k
        conv = conv + conv_w_ref[k:k + 1, :] * ubuf[:, lo:lo + L, :]
    a = (proj(SEG_BA) * conv.reshape(M, D)) * jax.nn.silu(proj(SEG_ZA))
    ya = jnp.dot(a.astype(bf16), w_a_ref[...], preferred_element_type=f32)
    m = jax.nn.sigmoid(proj(SEG_GA)) * ya

    xb = proj(SEG_XB).reshape(S, L, D)
    xbuf[:, POOL_PAD:POOL_PAD + L, :] = xb
    pos = lax.broadcasted_iota(jnp.int32, (S, L, POOL_GW), 1) + (past_len + t * L)
    mixed = []
    for g, w in enumerate(POOL_WINDOWS):
        cols = slice(g * POOL_GW, (g + 1) * POOL_GW)
        xg = xb[:, :, cols]
        s = xg
        for j in range(1, w):
            s = s + xbuf[:, POOL_PAD - j:POOL_PAD - j + L, cols]
        cnt = jnp.minimum(pos + 1, w).astype(f32)
        d = (s / cnt - xg).reshape(M, POOL_GW)
        mixed.append(jnp.dot(d.astype(bf16), w_grp_ref[g], preferred_element_type=f32))
    mixed = jnp.concatenate(mixed, axis=-1) * pool_scale_ref[...]
    b = mixed * jax.nn.silu(proj(SEG_ZB))
    yb = jnp.dot(b.astype(bf16), w_b_ref[...], preferred_element_type=f32)
    m = m + jax.nn.sigmoid(proj(SEG_GB)) * yb

    o = jnp.dot(m.astype(bf16), w_o_ref[...], preferred_element_type=f32)
    x1 = x + _rmsnorm(o, g_post_ref[...])
    pe = jnp.dot(p_ref[...].reshape(M, p_ref.shape[-1]).astype(bf16), w_ple_ref[...],
                 preferred_element_type=f32)
    gate = jnp.dot(x1.astype(bf16), w_pgate_ref[...], preferred_element_type=f32)
    y_ref[...] = (x1 + pe * jax.nn.sigmoid(gate)).reshape(S, L, D)

    new_conv = ubuf[:, CONV_PAD + L - (CONV_W - 1):CONV_PAD + L, :]
    new_pool = xbuf[:, POOL_PAD + L - POOL_HIST:POOL_PAD + L, :]
    nc_ref[...] = new_conv
    np_ref[...] = new_pool
    ubuf[:, CONV_PAD - (CONV_W - 1):CONV_PAD, :] = new_conv
    xbuf[:, POOL_PAD - POOL_HIST:POOL_PAD, :] = new_pool


def _layer(x, p, conv_hist, pool_hist, weights, *, streams, rows, past_len):
    B, T, D = x.shape
    S, L = streams, rows
    assert B % S == 0 and T % L == 0 and L % 8 == 0 and L >= POOL_HIST
    grid = (B // S, T // L)

    def resident(arr):
        return pl.BlockSpec(arr.shape, lambda b, t: (0,) * arr.ndim,
                            pipeline_mode=pl.Buffered(1))

    def per_stream(n_rows):
        return pl.BlockSpec((S, n_rows, D), lambda b, t: (b, 0, 0))

    tile = lambda width: pl.BlockSpec((S, L, width), lambda b, t: (b, t, 0))
    f32 = jnp.float32
    return pl.pallas_call(
        functools.partial(_layer_kernel, past_len=past_len),
        grid=grid,
        in_specs=[tile(D), tile(p.shape[-1]), per_stream(CONV_W - 1), per_stream(POOL_HIST)]
        + [resident(w) for w in weights],
        out_specs=[tile(D), per_stream(CONV_W - 1), per_stream(POOL_HIST)],
        out_shape=[jax.ShapeDtypeStruct((B, T, D), f32),
                   jax.ShapeDtypeStruct((B, CONV_W - 1, D), f32),
                   jax.ShapeDtypeStruct((B, POOL_HIST, D), f32)],
        scratch_shapes=[pltpu.VMEM((S, CONV_PAD + L, D), f32),
                        pltpu.VMEM((S, POOL_PAD + L, D), f32)],
        compiler_params=pltpu.CompilerParams(
            dimension_semantics=("arbitrary", "arbitrary"),
            vmem_limit_bytes=V7X_VMEM_LIMIT_BYTES),
    )(x, p, conv_hist, pool_hist, *weights)


def kernel(x_prompt, x_sample, p_prompt, p_sample, cache_conv, state_pool, g_pre, w_in,
           conv_w, conv_b, w_grp, pool_scale, w_a_out, w_b_out, w_o, g_post, w_ple, w_pgate):
    depth = w_in.shape[0]
    bf16 = jnp.bfloat16
    bp = x_prompt.shape[0]
    hp, hs = x_prompt, x_sample
    zc = jnp.zeros((bp, CONV_W - 1, D_MODEL), hp.dtype)
    zp = jnp.zeros((bp, POOL_HIST, D_MODEL), hp.dtype)
    conv_p, conv_s, pool_p, pool_s = [], [], [], []
    for i in range(depth):
        weights = (g_pre[i][None, :], w_in[i].astype(bf16), conv_w[i], conv_b[i][None, :],
                   w_grp[i].astype(bf16), pool_scale[i][None, :], w_a_out[i].astype(bf16),
                   w_b_out[i].astype(bf16), w_o[i].astype(bf16), g_post[i][None, :],
                   w_ple[i].astype(bf16), w_pgate[i].astype(bf16))
        hp, nc_p, np_p = _layer(hp, p_prompt[i], zc, zp, weights,
                                streams=1, rows=PROMPT_TILE, past_len=0)
        hs, nc_s, np_s = _layer(hs, p_sample[i], cache_conv[i], state_pool[i], weights,
                                streams=SAMPLE_STREAMS, rows=hs.shape[1], past_len=PAST_LEN)
        conv_p.append(nc_p); pool_p.append(np_p)
        conv_s.append(nc_s); pool_s.append(np_s)
    return (hp, hs, jnp.stack(conv_p), jnp.stack(conv_s), jnp.stack(pool_p), jnp.stack(pool_s))
```

```python
import functools

import jax
import jax.numpy as jnp
from jax import lax
from jax.experimental import pallas as pl
from jax.experimental.pallas import tpu as pltpu

D_MODEL = 1024
CONV_W = 3
POOL_WINDOWS = (2, 4, 8, 16)
N_POOL_GROUPS = len(POOL_WINDOWS)
POOL_GW = D_MODEL // N_POOL_GROUPS
POOL_HIST = max(POOL_WINDOWS) - 1
PLE_DIM = 256
PAST_LEN = 4096
EPS = 1e-6

SEG_XA, SEG_BA, SEG_CA, SEG_ZA, SEG_XB, SEG_ZB, SEG_GA, SEG_GB = range(8)

CONV_PAD = 8
POOL_PAD = 16
assert CONV_PAD >= CONV_W - 1 and POOL_PAD >= POOL_HIST

V7X_VMEM_LIMIT_BYTES = 56 * 1024 * 1024

PROMPT_TILE = (1, 512)
SAMPLE_TILE = (16, 32)
CHAIN_ROWS = 256


def _rmsnorm(x, g):
    ms = jnp.mean(x * x, axis=-1, keepdims=True)
    return x * lax.rsqrt(ms + EPS) * g


def _chain(s0, r0, Ss, R, pos0, x_ref, p_ref, g_pre_ref, w_in_ref, conv_w_ref, conv_b_ref,
           w_grp_ref, pool_scale_ref, w_a_ref, w_b_ref, w_o_ref, g_post_ref, w_ple_ref,
           w_pgate_ref, y_ref, ubuf, xbuf):
    D = x_ref.shape[-1]
    M = Ss * R
    f32, bf16 = jnp.float32, jnp.bfloat16
    streams = slice(s0, s0 + Ss)

    x = x_ref[streams, r0:r0 + R, :].reshape(M, D)
    h = _rmsnorm(x, g_pre_ref[...]).astype(bf16)

    def proj(seg):
        return jnp.dot(h, w_in_ref[:, seg * D:(seg + 1) * D], preferred_element_type=f32)

    xb = proj(SEG_XB).reshape(Ss, R, D)
    xbuf[streams, POOL_PAD + r0:POOL_PAD + r0 + R, :] = xb
    u = (proj(SEG_CA) * proj(SEG_XA)).reshape(Ss, R, D)
    ubuf[streams, CONV_PAD + r0:CONV_PAD + r0 + R, :] = u

    pos = lax.broadcasted_iota(jnp.int32, (Ss, R, POOL_GW), 1) + (pos0 + r0)
    d = []
    for g, w in enumerate(POOL_WINDOWS):
        cols = slice(g * POOL_GW, (g + 1) * POOL_GW)
        xg = xb[:, :, cols]
        s = xg
        for j in range(1, w):
            lo = POOL_PAD + r0 - j
            s = s + xbuf[streams, lo:lo + R, cols]
        cnt = jnp.minimum(pos + 1, w).astype(f32)
        d.append((s / cnt - xg).reshape(M, POOL_GW).astype(bf16))

    conv = conv_b_ref[...] + conv_w_ref[CONV_W - 1:CONV_W, :] * u
    for k in range(CONV_W - 1):
        lo = CONV_PAD + r0 - (CONV_W - 1) + k
        conv = conv + conv_w_ref[k:k + 1, :] * ubuf[streams, lo:lo + R, :]
    a = (proj(SEG_BA) * conv.reshape(M, D)) * jax.nn.silu(proj(SEG_ZA))
    ya = jnp.dot(a.astype(bf16), w_a_ref[...], preferred_element_type=f32)

    mixed = [jnp.dot(d[g], w_grp_ref[g], preferred_element_type=f32)
             for g in range(N_POOL_GROUPS)]
    mixed = jnp.concatenate(mixed, axis=-1) * pool_scale_ref[...]
    b = mixed * jax.nn.silu(proj(SEG_ZB))
    yb = jnp.dot(b.astype(bf16), w_b_ref[...], preferred_element_type=f32)
    m = jax.nn.sigmoid(proj(SEG_GA)) * ya + jax.nn.sigmoid(proj(SEG_GB)) * yb

    o = jnp.dot(m.astype(bf16), w_o_ref[...], preferred_element_type=f32)
    x1 = x + _rmsnorm(o, g_post_ref[...])
    p = p_ref[streams, r0:r0 + R, :].reshape(M, p_ref.shape[-1])
    pe = jnp.dot(p.astype(bf16), w_ple_ref[...], preferred_element_type=f32)
    gate = jnp.dot(x1.astype(bf16), w_pgate_ref[...], preferred_element_type=f32)
    y_ref[streams, r0:r0 + R, :] = (x1 + pe * jax.nn.sigmoid(gate)).reshape(Ss, R, D)


def _layer_kernel(x_ref, p_ref, ch_ref, ph_ref, g_pre_ref, w_in_ref, conv_w_ref,
                  conv_b_ref, w_grp_ref, pool_scale_ref, w_a_ref, w_b_ref, w_o_ref,
                  g_post_ref, w_ple_ref, w_pgate_ref,
                  y_ref, nc_ref, np_ref, ubuf, xbuf, *, past_len, chains):
    S, L, _ = x_ref.shape
    Ss, R = chains
    t = pl.program_id(1)

    @pl.when(t == 0)
    def _():
        ubuf[:, CONV_PAD - (CONV_W - 1):CONV_PAD, :] = ch_ref[...]
        xbuf[:, POOL_PAD - POOL_HIST:POOL_PAD, :] = ph_ref[...]

    for s0 in range(0, S, Ss):
        for r0 in range(0, L, R):
            _chain(s0, r0, Ss, R, past_len + t * L, x_ref, p_ref, g_pre_ref, w_in_ref,
                   conv_w_ref, conv_b_ref, w_grp_ref, pool_scale_ref, w_a_ref, w_b_ref,
                   w_o_ref, g_post_ref, w_ple_ref, w_pgate_ref, y_ref, ubuf, xbuf)

    new_conv = ubuf[:, CONV_PAD + L - (CONV_W - 1):CONV_PAD + L, :]
    new_pool = xbuf[:, POOL_PAD + L - POOL_HIST:POOL_PAD + L, :]
    nc_ref[...] = new_conv
    np_ref[...] = new_pool
    ubuf[:, CONV_PAD - (CONV_W - 1):CONV_PAD, :] = new_conv
    xbuf[:, POOL_PAD - POOL_HIST:POOL_PAD, :] = new_pool


def _chain_shape(S, L):
    if L >= CHAIN_ROWS:
        assert L % CHAIN_ROWS == 0
        return 1, CHAIN_ROWS
    assert CHAIN_ROWS % L == 0
    return min(S, CHAIN_ROWS // L), L


def _layer(x, p, conv_hist, pool_hist, weights, *, tile, past_len):
    B, T, D = x.shape
    S, L = tile
    chains = _chain_shape(S, L)
    assert B % S == 0 and T % L == 0 and S % chains[0] == 0 and L >= POOL_HIST
    grid = (B // S, T // L)

    def resident(arr):
        return pl.BlockSpec(arr.shape, lambda b, t: (0,) * arr.ndim,
                            pipeline_mode=pl.Buffered(1))

    def per_stream(n_rows):
        return pl.BlockSpec((S, n_rows, D), lambda b, t: (b, 0, 0))

    def row_tile(width):
        return pl.BlockSpec((S, L, width), lambda b, t: (b, t, 0))

    f32 = jnp.float32
    return pl.pallas_call(
        functools.partial(_layer_kernel, past_len=past_len, chains=chains),
        grid=grid,
        in_specs=[row_tile(D), row_tile(p.shape[-1]), per_stream(CONV_W - 1),
                  per_stream(POOL_HIST)] + [resident(w) for w in weights],
        out_specs=[row_tile(D), per_stream(CONV_W - 1), per_stream(POOL_HIST)],
        out_shape=[jax.ShapeDtypeStruct((B, T, D), f32),
                   jax.ShapeDtypeStruct((B, CONV_W - 1, D), f32),
                   jax.ShapeDtypeStruct((B, POOL_HIST, D), f32)],
        scratch_shapes=[pltpu.VMEM((S, CONV_PAD + L, D), f32),
                        pltpu.VMEM((S, POOL_PAD + L, D), f32)],
        compiler_params=pltpu.CompilerParams(
            dimension_semantics=("arbitrary", "arbitrary"),
            vmem_limit_bytes=V7X_VMEM_LIMIT_BYTES),
    )(x, p, conv_hist, pool_hist, *weights)


def kernel(x_prompt, x_sample, p_prompt, p_sample, cache_conv, state_pool, g_pre, w_in,
           conv_w, conv_b, w_grp, pool_scale, w_a_out, w_b_out, w_o, g_post, w_ple, w_pgate):
    depth = w_in.shape[0]
    bf16 = jnp.bfloat16
    bp = x_prompt.shape[0]
    hp, hs = x_prompt, x_sample
    zc = jnp.zeros((bp, CONV_W - 1, D_MODEL), hp.dtype)
    zp = jnp.zeros((bp, POOL_HIST, D_MODEL), hp.dtype)
    conv_p, conv_s, pool_p, pool_s = [], [], [], []
    for i in range(depth):
        weights = (g_pre[i][None, :], w_in[i].astype(bf16), conv_w[i], conv_b[i][None, :],
                   w_grp[i].astype(bf16), pool_scale[i][None, :], w_a_out[i].astype(bf16),
                   w_b_out[i].astype(bf16), w_o[i].astype(bf16), g_post[i][None, :],
                   w_ple[i].astype(bf16), w_pgate[i].astype(bf16))
        hp, nc_p, np_p = _layer(hp, p_prompt[i], zc, zp, weights,
                                tile=PROMPT_TILE, past_len=0)
        hs, nc_s, np_s = _layer(hs, p_sample[i], cache_conv[i], state_pool[i], weights,
                                tile=SAMPLE_TILE, past_len=PAST_LEN)
        conv_p.append(nc_p); pool_p.append(np_p)
        conv_s.append(nc_s); pool_s.append(np_s)
    return (hp, hs, jnp.stack(conv_p), jnp.stack(conv_s), jnp.stack(pool_p), jnp.stack(pool_s))
```

```python
import functools

import jax
import jax.numpy as jnp
from jax import lax
from jax.experimental import pallas as pl
from jax.experimental.pallas import tpu as pltpu

D_MODEL = 1024
CONV_W = 3
POOL_WINDOWS = (2, 4, 8, 16)
N_POOL_GROUPS = len(POOL_WINDOWS)
POOL_GW = D_MODEL // N_POOL_GROUPS
POOL_HIST = max(POOL_WINDOWS) - 1
PLE_DIM = 256
PAST_LEN = 4096
EPS = 1e-6

SEG_XA, SEG_BA, SEG_CA, SEG_ZA, SEG_XB, SEG_ZB, SEG_GA, SEG_GB = range(8)

CONV_PAD = 8
POOL_PAD = 16
assert CONV_PAD >= CONV_W - 1 and POOL_PAD >= POOL_HIST

V7X_VMEM_LIMIT_BYTES = 56 * 1024 * 1024

PROMPT_TILE = (1, 512)
SAMPLE_TILE = (16, 32)
CHAIN_ROWS = 256


def _rmsnorm(x, g):
    ms = jnp.mean(x * x, axis=-1, keepdims=True)
    return x * lax.rsqrt(ms + EPS) * g


def _shift_rows(v, k):
    return pltpu.roll(v, k, axis=v.ndim - 2)


def _trailing_sum(v, w):
    assert w & (w - 1) == 0
    k = 1
    while k < w:
        v = v + _shift_rows(v, k)
        k *= 2
    return v


def _chain(s0, r0, Ss, R, pos0, x_ref, p_ref, g_pre_ref, w_in_ref, conv_w_ref, conv_b_ref,
           w_grp_ref, pool_scale_ref, w_a_ref, w_b_ref, w_o_ref, g_post_ref, w_ple_ref,
           w_pgate_ref, y_ref, ubuf, xbuf):
    D = x_ref.shape[-1]
    M = Ss * R
    f32, bf16 = jnp.float32, jnp.bfloat16
    streams = slice(s0, s0 + Ss)

    x = x_ref[streams, r0:r0 + R, :].reshape(M, D)
    h = _rmsnorm(x, g_pre_ref[...]).astype(bf16)
    yield

    def proj(seg):
        return jnp.dot(h, w_in_ref[:, seg * D:(seg + 1) * D], preferred_element_type=f32)

    xb = proj(SEG_XB).reshape(Ss, R, D)
    yield
    xbuf[streams, POOL_PAD + r0:POOL_PAD + r0 + R, :] = xb
    ca = proj(SEG_CA)
    yield
    u = (ca * proj(SEG_XA)).reshape(Ss, R, D)
    yield
    ubuf[streams, CONV_PAD + r0:CONV_PAD + r0 + R, :] = u

    pos = lax.broadcasted_iota(jnp.int32, (Ss, R, POOL_GW), 1) + (pos0 + r0)
    d = []
    for g, w in enumerate(POOL_WINDOWS):
        cols = slice(g * POOL_GW, (g + 1) * POOL_GW)
        xg = xb[:, :, cols]
        ext = jnp.concatenate([xbuf[streams, r0:r0 + POOL_PAD, cols], xg], axis=1)
        s = _trailing_sum(ext, w)[:, POOL_PAD:, :]
        cnt = jnp.minimum(pos + 1, w).astype(f32)
        d.append((s / cnt - xg).reshape(M, POOL_GW).astype(bf16))

    ext = jnp.concatenate([ubuf[streams, r0:r0 + CONV_PAD, :], u], axis=1)
    conv = conv_b_ref[...] + conv_w_ref[CONV_W - 1:CONV_W, :] * u
    for k in range(CONV_W - 1):
        shifted = _shift_rows(ext, CONV_W - 1 - k)[:, CONV_PAD:, :]
        conv = conv + conv_w_ref[k:k + 1, :] * shifted
    ba = proj(SEG_BA)
    yield
    a = ((ba * conv.reshape(M, D)) * jax.nn.silu(proj(SEG_ZA))).astype(bf16)
    yield
    zb = proj(SEG_ZB)
    yield
    mixed = []
    for g in range(N_POOL_GROUPS):
        mixed.append(jnp.dot(d[g], w_grp_ref[g], preferred_element_type=f32))
        yield
    mixed = jnp.concatenate(mixed, axis=-1) * pool_scale_ref[...]
    b = (mixed * jax.nn.silu(zb)).astype(bf16)
    ya = jnp.dot(a, w_a_ref[...], preferred_element_type=f32)
    yield
    m = jax.nn.sigmoid(proj(SEG_GA)) * ya
    yield
    yb = jnp.dot(b, w_b_ref[...], preferred_element_type=f32)
    yield _TAIL
    m = (m + jax.nn.sigmoid(proj(SEG_GB)) * yb).astype(bf16)
    yield
    p = p_ref[streams, r0:r0 + R, :].reshape(M, p_ref.shape[-1])
    pe = jnp.dot(p.astype(bf16), w_ple_ref[...], preferred_element_type=f32)
    yield

    o = jnp.dot(m, w_o_ref[...], preferred_element_type=f32)
    yield
    x1 = x + _rmsnorm(o, g_post_ref[...])
    gate = jnp.dot(x1.astype(bf16), w_pgate_ref[...], preferred_element_type=f32)
    yield
    y_ref[streams, r0:r0 + R, :] = (x1 + pe * jax.nn.sigmoid(gate)).reshape(Ss, R, D)


_TAIL = "tail"


def _run_interleaved(chains):
    prev = None
    for cur in chains:
        for marker in cur:
            if prev is not None and next(prev, _TAIL) is _TAIL:
                prev = None
            if marker is _TAIL:
                break
        if prev is not None:
            for _ in prev:
                pass
        prev = cur
    for _ in prev:
        pass


def _layer_kernel(x_ref, p_ref, ch_ref, ph_ref, g_pre_ref, w_in_ref, conv_w_ref,
                  conv_b_ref, w_grp_ref, pool_scale_ref, w_a_ref, w_b_ref, w_o_ref,
                  g_post_ref, w_ple_ref, w_pgate_ref,
                  y_ref, nc_ref, np_ref, ubuf, xbuf, *, past_len, chains):
    S, L, _ = x_ref.shape
    Ss, R = chains
    t = pl.program_id(1)

    @pl.when(t == 0)
    def _():
        ubuf[:, 0:CONV_PAD, :] = jnp.zeros((S, CONV_PAD, ubuf.shape[-1]), ubuf.dtype)
        xbuf[:, 0:POOL_PAD, :] = jnp.zeros((S, POOL_PAD, xbuf.shape[-1]), xbuf.dtype)
        ubuf[:, CONV_PAD - (CONV_W - 1):CONV_PAD, :] = ch_ref[...]
        xbuf[:, POOL_PAD - POOL_HIST:POOL_PAD, :] = ph_ref[...]

    _run_interleaved(
        _chain(s0, r0, Ss, R, past_len + t * L, x_ref, p_ref, g_pre_ref, w_in_ref,
               conv_w_ref, conv_b_ref, w_grp_ref, pool_scale_ref, w_a_ref, w_b_ref,
               w_o_ref, g_post_ref, w_ple_ref, w_pgate_ref, y_ref, ubuf, xbuf)
        for s0 in range(0, S, Ss) for r0 in range(0, L, R))

    new_conv = ubuf[:, CONV_PAD + L - (CONV_W - 1):CONV_PAD + L, :]
    new_pool = xbuf[:, POOL_PAD + L - POOL_HIST:POOL_PAD + L, :]
    nc_ref[...] = new_conv
    np_ref[...] = new_pool
    ubuf[:, CONV_PAD - (CONV_W - 1):CONV_PAD, :] = new_conv
    xbuf[:, POOL_PAD - POOL_HIST:POOL_PAD, :] = new_pool


def _chain_shape(S, L):
    if L >= CHAIN_ROWS:
        assert L % CHAIN_ROWS == 0
        return 1, CHAIN_ROWS
    assert CHAIN_ROWS % L == 0
    return min(S, CHAIN_ROWS // L), L


def _layer(x, p, conv_hist, pool_hist, weights, *, tile, past_len):
    B, T, D = x.shape
    S, L = tile
    chains = _chain_shape(S, L)
    assert B % S == 0 and T % L == 0 and S % chains[0] == 0 and L >= POOL_HIST
    grid = (B // S, T // L)

    def resident(arr):
        return pl.BlockSpec(arr.shape, lambda b, t: (0,) * arr.ndim,
                            pipeline_mode=pl.Buffered(1))

    def per_stream(n_rows):
        return pl.BlockSpec((S, n_rows, D), lambda b, t: (b, 0, 0))

    def row_tile(width):
        return pl.BlockSpec((S, L, width), lambda b, t: (b, t, 0))

    f32 = jnp.float32
    return pl.pallas_call(
        functools.partial(_layer_kernel, past_len=past_len, chains=chains),
        grid=grid,
        in_specs=[row_tile(D), row_tile(p.shape[-1]), per_stream(CONV_W - 1),
                  per_stream(POOL_HIST)] + [resident(w) for w in weights],
        out_specs=[row_tile(D), per_stream(CONV_W - 1), per_stream(POOL_HIST)],
        out_shape=[jax.ShapeDtypeStruct((B, T, D), f32),
                   jax.ShapeDtypeStruct((B, CONV_W - 1, D), f32),
                   jax.ShapeDtypeStruct((B, POOL_HIST, D), f32)],
        scratch_shapes=[pltpu.VMEM((S, CONV_PAD + L, D), f32),
                        pltpu.VMEM((S, POOL_PAD + L, D), f32)],
        compiler_params=pltpu.CompilerParams(
            dimension_semantics=("arbitrary", "arbitrary"),
            vmem_limit_bytes=V7X_VMEM_LIMIT_BYTES),
    )(x, p, conv_hist, pool_hist, *weights)


def kernel(x_prompt, x_sample, p_prompt, p_sample, cache_conv, state_pool, g_pre, w_in,
           conv_w, conv_b, w_grp, pool_scale, w_a_out, w_b_out, w_o, g_post, w_ple, w_pgate):
    depth = w_in.shape[0]
    bf16 = jnp.bfloat16
    bp = x_prompt.shape[0]
    hp, hs = x_prompt, x_sample
    zc = jnp.zeros((bp, CONV_W - 1, D_MODEL), hp.dtype)
    zp = jnp.zeros((bp, POOL_HIST, D_MODEL), hp.dtype)
    conv_p, conv_s, pool_p, pool_s = [], [], [], []
    for i in range(depth):
        weights = (g_pre[i][None, :], w_in[i].astype(bf16), conv_w[i], conv_b[i][None, :],
                   w_grp[i].astype(bf16), pool_scale[i][None, :], w_a_out[i].astype(bf16),
                   w_b_out[i].astype(bf16), w_o[i].astype(bf16), g_post[i][None, :],
                   w_ple[i].astype(bf16), w_pgate[i].astype(bf16))
        hp, nc_p, np_p = _layer(hp, p_prompt[i], zc, zp, weights,
                                tile=PROMPT_TILE, past_len=0)
        hs, nc_s, np_s = _layer(hs, p_sample[i], cache_conv[i], state_pool[i], weights,
                                tile=SAMPLE_TILE, past_len=PAST_LEN)
        conv_p.append(nc_p); pool_p.append(np_p)
        conv_s.append(nc_s); pool_s.append(np_s)
    return (hp, hs, jnp.stack(conv_p), jnp.stack(conv_s), jnp.stack(pool_p), jnp.stack(pool_s))
```

```python
import functools
from typing import Any, NamedTuple

import jax
import jax.numpy as jnp
from jax import lax
from jax.experimental import pallas as pl
from jax.experimental.pallas import tpu as pltpu

D_MODEL = 1024
CONV_W = 3
POOL_WINDOWS = (2, 4, 8, 16)
N_POOL_GROUPS = len(POOL_WINDOWS)
POOL_GW = D_MODEL // N_POOL_GROUPS
POOL_HIST = max(POOL_WINDOWS) - 1
PAST_LEN = 4096
EPS = 1e-6

SEG_XA, SEG_BA, SEG_CA, SEG_ZA, SEG_XB, SEG_ZB, SEG_GA, SEG_GB = range(8)

CONV_PAD = 8
POOL_PAD = 16
assert CONV_PAD >= CONV_W - 1 and POOL_PAD >= POOL_HIST

V7X_VMEM_BYTES = 64 * 1024 * 1024
VMEM_RESERVE_BYTES = 4 * 1024 * 1024

PROMPT_TILE = (1, 512)
SAMPLE_TILE = (8, 32)
CHAIN_ROWS = 256

STAGE_ROWS, STAGE_COLS, STAGE_SLOTS = 256, 1024, 4


class _Weights(NamedTuple):
    g_pre: Any
    conv_w: Any
    conv_b: Any
    pool_scale: Any
    g_post: Any
    w_in: Any
    w_grp: Any
    w_a: Any
    w_b: Any
    w_o: Any
    w_ple: Any
    w_pgate: Any


def _rmsnorm(x, g):
    ms = jnp.mean(x * x, axis=-1, keepdims=True)
    return x * lax.rsqrt(ms + EPS) * g


def _shift_rows(v, k):
    return pltpu.roll(v, k, axis=v.ndim - 2)


def _trailing_sum(v, w):
    assert w & (w - 1) == 0
    k = 1
    while k < w:
        v = v + _shift_rows(v, k)
        k *= 2
    return v


_TAIL = "tail"


def _chain(s0, r0, Ss, R, pos0, x_ref, p_ref, y_ref, ubuf, xbuf, W):
    D = x_ref.shape[-1]
    M = Ss * R
    f32, bf16 = jnp.float32, jnp.bfloat16
    streams = slice(s0, s0 + Ss)

    x = x_ref[streams, r0:r0 + R, :].reshape(M, D)
    h = _rmsnorm(x, W.g_pre[...]).astype(bf16)
    yield

    def proj(seg):
        return jnp.dot(h, W.w_in[:, seg * D:(seg + 1) * D], preferred_element_type=f32)

    xb = proj(SEG_XB).reshape(Ss, R, D)
    yield
    xbuf[streams, POOL_PAD + r0:POOL_PAD + r0 + R, :] = xb
    ca = proj(SEG_CA)
    yield
    u = (ca * proj(SEG_XA)).reshape(Ss, R, D)
    yield
    ubuf[streams, CONV_PAD + r0:CONV_PAD + r0 + R, :] = u

    pos = lax.broadcasted_iota(jnp.int32, (Ss, R, POOL_GW), 1) + (pos0 + r0)
    d = []
    for g, w in enumerate(POOL_WINDOWS):
        cols = slice(g * POOL_GW, (g + 1) * POOL_GW)
        xg = xb[:, :, cols]
        ext = jnp.concatenate([xbuf[streams, r0:r0 + POOL_PAD, cols], xg], axis=1)
        s = _trailing_sum(ext, w)[:, POOL_PAD:, :]
        cnt = jnp.minimum(pos + 1, w).astype(f32)
        d.append((s / cnt - xg).reshape(M, POOL_GW).astype(bf16))

    ext = jnp.concatenate([ubuf[streams, r0:r0 + CONV_PAD, :], u], axis=1)
    conv = W.conv_b[...] + W.conv_w[CONV_W - 1:CONV_W, :] * u
    for k in range(CONV_W - 1):
        shifted = _shift_rows(ext, CONV_W - 1 - k)[:, CONV_PAD:, :]
        conv = conv + W.conv_w[k:k + 1, :] * shifted
    ba = proj(SEG_BA)
    yield
    a = ((ba * conv.reshape(M, D)) * jax.nn.silu(proj(SEG_ZA))).astype(bf16)
    yield
    zb = proj(SEG_ZB)
    yield
    mixed = []
    for g in range(N_POOL_GROUPS):
        mixed.append(jnp.dot(d[g], W.w_grp[g], preferred_element_type=f32))
        yield
    mixed = jnp.concatenate(mixed, axis=-1) * W.pool_scale[...]
    b = (mixed * jax.nn.silu(zb)).astype(bf16)
    ya = jnp.dot(a, W.w_a[...], preferred_element_type=f32)
    yield
    m = jax.nn.sigmoid(proj(SEG_GA)) * ya
    yield
    yb = jnp.dot(b, W.w_b[...], preferred_element_type=f32)
    yield _TAIL
    m = (m + jax.nn.sigmoid(proj(SEG_GB)) * yb).astype(bf16)
    yield
    p = p_ref[streams, r0:r0 + R, :].reshape(M, p_ref.shape[-1])
    pe = jnp.dot(p.astype(bf16), W.w_ple[...], preferred_element_type=f32)
    yield

    o = jnp.dot(m, W.w_o[...], preferred_element_type=f32)
    yield
    x1 = x + _rmsnorm(o, W.g_post[...])
    gate = jnp.dot(x1.astype(bf16), W.w_pgate[...], preferred_element_type=f32)
    yield
    y_ref[streams, r0:r0 + R, :] = (x1 + pe * jax.nn.sigmoid(gate)).reshape(Ss, R, D)


def _run_interleaved(chains):
    prev = None
    for cur in chains:
        for marker in cur:
            if prev is not None and next(prev, _TAIL) is _TAIL:
                prev = None
            if marker is _TAIL:
                break
        if prev is not None:
            for _ in prev:
                pass
        prev = cur
    for _ in prev:
        pass


def _chain_shape(S, L):
    if L >= CHAIN_ROWS:
        assert L % CHAIN_ROWS == 0
        return 1, CHAIN_ROWS
    assert CHAIN_ROWS % L == 0
    return min(S, CHAIN_ROWS // L), L


def _tile(x_ref, p_ref, y_ref, nc_ref, np_ref, ubuf, xbuf, W, *, t, pos0, history):
    S, L, _ = x_ref.shape
    Ss, R = _chain_shape(S, L)

    @pl.when(t == 0)
    def _():
        ubuf[:, 0:CONV_PAD, :] = jnp.zeros((S, CONV_PAD, ubuf.shape[-1]), ubuf.dtype)
        xbuf[:, 0:POOL_PAD, :] = jnp.zeros((S, POOL_PAD, xbuf.shape[-1]), xbuf.dtype)
        if history is not None:
            conv_hist_ref, pool_hist_ref = history
            ubuf[:, CONV_PAD - (CONV_W - 1):CONV_PAD, :] = conv_hist_ref[...]
            xbuf[:, POOL_PAD - POOL_HIST:POOL_PAD, :] = pool_hist_ref[...]

    _run_interleaved(_chain(s0, r0, Ss, R, pos0, x_ref, p_ref, y_ref, ubuf, xbuf, W)
                     for s0 in range(0, S, Ss) for r0 in range(0, L, R))

    new_conv = ubuf[:, CONV_PAD + L - (CONV_W - 1):CONV_PAD + L, :]
    new_pool = xbuf[:, POOL_PAD + L - POOL_HIST:POOL_PAD + L, :]
    nc_ref[...] = new_conv
    np_ref[...] = new_pool
    ubuf[:, CONV_PAD - (CONV_W - 1):CONV_PAD, :] = new_conv
    xbuf[:, POOL_PAD - POOL_HIST:POOL_PAD, :] = new_pool


def _weight_chunks(layer, hbm, vmem):
    chunks = []
    for src, dst in zip(hbm, vmem):
        if dst.ndim == 3:
            for g in range(dst.shape[0]):
                chunks.append((src.at[layer, g], dst.at[g], dst.shape[2]))
            continue
        rows, cols = dst.shape
        for c0 in range(0, cols, STAGE_COLS):
            for r0 in range(0, rows, STAGE_ROWS):
                window = (pl.ds(r0, STAGE_ROWS), pl.ds(c0, STAGE_COLS))
                chunks.append((src.at[(layer,) + window], dst.at[window], STAGE_COLS))
    return chunks


def _load_weights(layer, hbm, vmem, stage, sem):
    chunks = _weight_chunks(layer, hbm, vmem)

    def copy(i):
        src, _, cols = chunks[i]
        slot = i % STAGE_SLOTS
        return pltpu.make_async_copy(src, stage.at[slot, :, pl.ds(0, cols)], sem.at[slot])

    for i in range(min(STAGE_SLOTS, len(chunks))):
        copy(i).start()
    for i, (_, dst, cols) in enumerate(chunks):
        copy(i).wait()
        dst[...] = stage[i % STAGE_SLOTS, :, 0:cols].astype(dst.dtype)
        if i + STAGE_SLOTS < len(chunks):
            copy(i + STAGE_SLOTS).start()


def _fused_kernel(xs_ref, ps_ref, ch_ref, ph_ref, xp_ref, pp_ref,
                  g_pre_ref, conv_w_ref, conv_b_ref, pool_scale_ref, g_post_ref,
                  w_in_hbm, w_grp_hbm, w_a_hbm, w_b_hbm, w_o_hbm, w_ple_hbm, w_pgate_hbm,
                  ys_ref, ncs_ref, nps_ref, yp_ref, ncp_ref, npp_ref,
                  w_in, w_grp, w_a, w_b, w_o, w_ple, w_pgate, stage, sem,
                  ubuf_s, xbuf_s, ubuf_p, xbuf_p, *, layer, sample_steps, prompt_tiles_per_seq):
    j = pl.program_id(0)
    hbm = (w_in_hbm, w_grp_hbm, w_a_hbm, w_b_hbm, w_o_hbm, w_ple_hbm, w_pgate_hbm)
    vmem = (w_in, w_grp, w_a, w_b, w_o, w_ple, w_pgate)

    @pl.when(j == 0)
    def _():
        _load_weights(layer, hbm, vmem, stage, sem)

    W = _Weights(g_pre_ref, conv_w_ref, conv_b_ref, pool_scale_ref, g_post_ref, *vmem)

    @pl.when(j < sample_steps)
    def _():
        _tile(xs_ref, ps_ref, ys_ref, ncs_ref, nps_ref, ubuf_s, xbuf_s, W,
              t=0, pos0=PAST_LEN, history=(ch_ref, ph_ref))

    @pl.when(j >= sample_steps)
    def _():
        t = lax.rem(j - sample_steps, prompt_tiles_per_seq)
        _tile(xp_ref, pp_ref, yp_ref, ncp_ref, npp_ref, ubuf_p, xbuf_p, W,
              t=t, pos0=t * xp_ref.shape[1], history=None)


def _nbytes(shape, dtype):
    n = jnp.dtype(dtype).itemsize
    for s in shape:
        n *= s
    return n


def _layer(layer, xs, ps, conv_hist, pool_hist, xp, pp, small, big):
    f32, bf16 = jnp.float32, jnp.bfloat16
    Bs, Ts, D = xs.shape
    Bp, Tp, _ = xp.shape
    Ss, Ls = SAMPLE_TILE
    Sp, Lp = PROMPT_TILE
    assert Ls == Ts and Bs % Ss == 0 and Sp == 1 and Tp % Lp == 0 and min(Ls, Lp) >= POOL_HIST
    sample_steps = Bs // Ss
    tiles_per_seq = Tp // Lp
    grid = (sample_steps + Bp * tiles_per_seq,)

    def sample_block(j):
        return jnp.minimum(j, sample_steps - 1)

    def prompt_block(j):
        q = jnp.maximum(j - sample_steps, 0)
        return q // tiles_per_seq, q % tiles_per_seq

    def sample_rows(width):
        return pl.BlockSpec((Ss, Ls, width), lambda j: (sample_block(j), 0, 0))

    def prompt_rows(width):
        return pl.BlockSpec((Sp, Lp, width), lambda j: prompt_block(j) + (0,))

    def sample_state(rows, depth_index):
        return pl.BlockSpec((None, Ss, rows, D), lambda j: (depth_index, sample_block(j), 0, 0))

    def prompt_state(rows):
        return pl.BlockSpec((None, Sp, rows, D), lambda j: (0, prompt_block(j)[0], 0, 0))

    def resident(arr):
        return pl.BlockSpec(arr.shape, lambda j: (0,) * arr.ndim, pipeline_mode=pl.Buffered(1))

    g_pre, conv_w, conv_b, pool_scale, g_post = small
    small_specs = [resident(g_pre),
                   pl.BlockSpec((None,) + conv_w.shape[1:], lambda j: (layer, 0, 0),
                                pipeline_mode=pl.Buffered(1)),
                   resident(conv_b), resident(pool_scale), resident(g_post)]
    in_specs = ([sample_rows(D), sample_rows(ps.shape[-1]), sample_state(CONV_W - 1, layer),
                 sample_state(POOL_HIST, layer), prompt_rows(D), prompt_rows(pp.shape[-1])]
                + small_specs + [pl.BlockSpec(memory_space=pl.ANY)] * len(big))
    out_specs = [sample_rows(D), sample_state(CONV_W - 1, 0), sample_state(POOL_HIST, 0),
                 prompt_rows(D), prompt_state(CONV_W - 1), prompt_state(POOL_HIST)]
    out_shape = [jax.ShapeDtypeStruct((Bs, Ts, D), f32),
                 jax.ShapeDtypeStruct((1, Bs, CONV_W - 1, D), f32),
                 jax.ShapeDtypeStruct((1, Bs, POOL_HIST, D), f32),
                 jax.ShapeDtypeStruct((Bp, Tp, D), f32),
                 jax.ShapeDtypeStruct((1, Bp, CONV_W - 1, D), f32),
                 jax.ShapeDtypeStruct((1, Bp, POOL_HIST, D), f32)]
    scratch = [(w.shape[1:], bf16) for w in big]
    scratch += [((STAGE_SLOTS, STAGE_ROWS, STAGE_COLS), f32)]
    buffers = [((Ss, CONV_PAD + Ls, D), f32), ((Ss, POOL_PAD + Ls, D), f32),
               ((Sp, CONV_PAD + Lp, D), f32), ((Sp, POOL_PAD + Lp, D), f32)]
    scratch_shapes = ([pltpu.VMEM(s, d) for s, d in scratch]
                      + [pltpu.SemaphoreType.DMA((STAGE_SLOTS,))]
                      + [pltpu.VMEM(s, d) for s, d in buffers])

    return pl.pallas_call(
        functools.partial(_fused_kernel, layer=layer, sample_steps=sample_steps,
                          prompt_tiles_per_seq=tiles_per_seq),
        grid=grid,
        in_specs=in_specs,
        out_specs=out_specs,
        out_shape=out_shape,
        scratch_shapes=scratch_shapes,
        compiler_params=pltpu.CompilerParams(
            dimension_semantics=("arbitrary",),
            vmem_limit_bytes=V7X_VMEM_BYTES - VMEM_RESERVE_BYTES),
    )(xs, ps, conv_hist, pool_hist, xp, pp, *small, *big)


def kernel(x_prompt, x_sample, p_prompt, p_sample, cache_conv, state_pool, g_pre, w_in,
           conv_w, conv_b, w_grp, pool_scale, w_a_out, w_b_out, w_o, g_post, w_ple, w_pgate):
    depth = w_in.shape[0]
    hp, hs = x_prompt, x_sample
    big = (w_in, w_grp, w_a_out, w_b_out, w_o, w_ple, w_pgate)
    conv_p, conv_s, pool_p, pool_s = [], [], [], []
    for i in range(depth):
        small = (g_pre[i][None, :], conv_w, conv_b[i][None, :], pool_scale[i][None, :],
                 g_post[i][None, :])
        hs, nc_s, np_s, hp, nc_p, np_p = _layer(i, hs, p_sample[i], cache_conv, state_pool,
                                                hp, p_prompt[i], small, big)
        conv_p.append(nc_p); pool_p.append(np_p)
        conv_s.append(nc_s); pool_s.append(np_s)

    def stacked(parts):
        return parts[0] if depth == 1 else jnp.concatenate(parts, axis=0)

    return (hp, hs, stacked(conv_p), stacked(conv_s), stacked(pool_p), stacked(pool_s))
```

```python
import functools
from typing import Any, NamedTuple

import jax
import jax.numpy as jnp
from jax import lax
from jax.experimental import pallas as pl
from jax.experimental.pallas import tpu as pltpu

D_MODEL = 1024
CONV_W = 3
POOL_WINDOWS = (2, 4, 8, 16)
N_POOL_GROUPS = len(POOL_WINDOWS)
POOL_GW = D_MODEL // N_POOL_GROUPS
POOL_HIST = max(POOL_WINDOWS) - 1
PAST_LEN = 4096
EPS = 1e-6

SEG_XA, SEG_BA, SEG_CA, SEG_ZA, SEG_XB, SEG_ZB, SEG_GA, SEG_GB = range(8)

CONV_PAD = 8
POOL_PAD = 16
assert CONV_PAD >= CONV_W - 1 and POOL_PAD >= POOL_HIST

SUBLANES, LANES = 8, 128
V7X_VMEM_BYTES = 64 * 1024 * 1024
LIVE_RESULTS_PER_CHAIN = 3

PROMPT_TILE = (1, 512)
SAMPLE_TILE = (8, 32)
CHAIN_ROWS = 256

STAGE_ROWS, STAGE_COLS, STAGE_SLOTS = 256, 1024, 4


class _Weights(NamedTuple):
    g_pre: Any
    conv_w: Any
    conv_b: Any
    pool_scale: Any
    g_post: Any
    w_in: Any
    w_grp: Any
    w_a: Any
    w_b: Any
    w_o: Any
    w_ple: Any
    w_pgate: Any


def _rmsnorm(x, g):
    ms = jnp.mean(x * x, axis=-1, keepdims=True)
    return x * lax.rsqrt(ms + EPS) * g


def _shift_rows(v, k):
    return pltpu.roll(v, k, axis=v.ndim - 2)


def _trailing_sum(v, w):
    assert w & (w - 1) == 0
    k = 1
    while k < w:
        v = v + _shift_rows(v, k)
        k *= 2
    return v


_TAIL = "tail"
_STAY = "stay"


def _chain(s0, r0, Ss, R, pos0, x_ref, p_ref, y_ref, ubuf, xbuf, W, embed_first):
    D = x_ref.shape[-1]
    M = Ss * R
    f32, bf16 = jnp.float32, jnp.bfloat16
    streams = slice(s0, s0 + Ss)

    def embed():
        p = p_ref[streams, r0:r0 + R, :].reshape(M, p_ref.shape[-1])
        return jnp.dot(p.astype(bf16), W.w_ple[...], preferred_element_type=f32)

    if embed_first:
        pe = embed()
        yield _STAY
    x = x_ref[streams, r0:r0 + R, :].reshape(M, D)
    h = _rmsnorm(x, W.g_pre[...]).astype(bf16)
    yield

    def proj(seg):
        return jnp.dot(h, W.w_in[:, seg * D:(seg + 1) * D], preferred_element_type=f32)

    xb = proj(SEG_XB).reshape(Ss, R, D)
    yield
    xbuf[streams, POOL_PAD + r0:POOL_PAD + r0 + R, :] = xb
    ca = proj(SEG_CA)
    yield
    u = (ca * proj(SEG_XA)).reshape(Ss, R, D)
    yield
    ubuf[streams, CONV_PAD + r0:CONV_PAD + r0 + R, :] = u

    pos = lax.broadcasted_iota(jnp.int32, (Ss, R, POOL_GW), 1) + (pos0 + r0)
    d = []
    for g, w in enumerate(POOL_WINDOWS):
        cols = slice(g * POOL_GW, (g + 1) * POOL_GW)
        xg = xb[:, :, cols]
        ext = jnp.concatenate([xbuf[streams, r0:r0 + POOL_PAD, cols], xg], axis=1)
        s = _trailing_sum(ext, w)[:, POOL_PAD:, :]
        cnt = jnp.minimum(pos + 1, w).astype(f32)
        d.append((s / cnt - xg).reshape(M, POOL_GW).astype(bf16))

    ext = jnp.concatenate([ubuf[streams, r0:r0 + CONV_PAD, :], u], axis=1)
    conv = W.conv_b[...] + W.conv_w[CONV_W - 1:CONV_W, :] * u
    for k in range(CONV_W - 1):
        shifted = _shift_rows(ext, CONV_W - 1 - k)[:, CONV_PAD:, :]
        conv = conv + W.conv_w[k:k + 1, :] * shifted
    ba = proj(SEG_BA)
    yield
    a = ((ba * conv.reshape(M, D)) * jax.nn.silu(proj(SEG_ZA))).astype(bf16)
    yield
    zb = proj(SEG_ZB)
    yield
    mixed = []
    for g in range(N_POOL_GROUPS):
        mixed.append(jnp.dot(d[g], W.w_grp[g], preferred_element_type=f32))
        yield
    mixed = jnp.concatenate(mixed, axis=-1) * W.pool_scale[...]
    b = (mixed * jax.nn.silu(zb)).astype(bf16)
    ya = jnp.dot(a, W.w_a[...], preferred_element_type=f32)
    yield
    m = jax.nn.sigmoid(proj(SEG_GA)) * ya
    yield
    yb = jnp.dot(b, W.w_b[...], preferred_element_type=f32)
    yield _TAIL
    m = (m + jax.nn.sigmoid(proj(SEG_GB)) * yb).astype(bf16)
    yield
    if not embed_first:
        pe = embed()
        yield

    n_parts = 1 if embed_first else 2
    parts = [(k * M // n_parts, (k + 1) * M // n_parts) for k in range(n_parts)]
    o = []
    for lo, hi in parts:
        o.append(jnp.dot(m[lo:hi], W.w_o[...], preferred_element_type=f32))
        yield
    for (lo, hi), o_part in zip(parts, o):
        x1 = x[lo:hi] + _rmsnorm(o_part, W.g_post[...])
        gate = jnp.dot(x1.astype(bf16), W.w_pgate[...], preferred_element_type=f32)
        yield
        y = x1 + pe[lo:hi] * jax.nn.sigmoid(gate)
        if Ss > 1:
            y_ref[s0 + lo // R:s0 + hi // R, r0:r0 + R, :] = y.reshape((hi - lo) // R, R, D)
        else:
            y_ref[streams, r0 + lo:r0 + hi, :] = y.reshape(1, hi - lo, D)


def _run_interleaved(chains):
    done = object()
    prev = None
    for cur in chains:
        for marker in cur:
            if marker is _STAY:
                continue
            if prev is not None and next(prev, done) is done:
                prev = None
            if marker is _TAIL:
                break
        if prev is not None:
            for _ in prev:
                pass
        prev = cur
    for _ in prev:
        pass


def _chain_shape(S, L):
    if L >= CHAIN_ROWS:
        assert L % CHAIN_ROWS == 0
        return 1, CHAIN_ROWS
    assert CHAIN_ROWS % L == 0
    return min(S, CHAIN_ROWS // L), L


def _tile(x_ref, p_ref, y_ref, nc_ref, np_ref, ubuf, xbuf, W, *, t, pos0, history):
    S, L, _ = x_ref.shape
    Ss, R = _chain_shape(S, L)

    @pl.when(t == 0)
    def _():
        ubuf[:, 0:CONV_PAD, :] = jnp.zeros((S, CONV_PAD, ubuf.shape[-1]), ubuf.dtype)
        xbuf[:, 0:POOL_PAD, :] = jnp.zeros((S, POOL_PAD, xbuf.shape[-1]), xbuf.dtype)
        if history is not None:
            conv_hist_ref, pool_hist_ref = history
            ubuf[:, CONV_PAD - (CONV_W - 1):CONV_PAD, :] = conv_hist_ref[...]
            xbuf[:, POOL_PAD - POOL_HIST:POOL_PAD, :] = pool_hist_ref[...]

    starts = [(s0, r0) for s0 in range(0, S, Ss) for r0 in range(0, L, R)]
    _run_interleaved(_chain(s0, r0, Ss, R, pos0, x_ref, p_ref, y_ref, ubuf, xbuf, W,
                            embed_first=i < len(starts) - 1)
                     for i, (s0, r0) in enumerate(starts))

    new_conv = ubuf[:, CONV_PAD + L - (CONV_W - 1):CONV_PAD + L, :]
    new_pool = xbuf[:, POOL_PAD + L - POOL_HIST:POOL_PAD + L, :]
    nc_ref[...] = new_conv
    np_ref[...] = new_pool
    ubuf[:, CONV_PAD - (CONV_W - 1):CONV_PAD, :] = new_conv
    xbuf[:, POOL_PAD - POOL_HIST:POOL_PAD, :] = new_pool


def _weight_chunks(layer, hbm, vmem):
    chunks = []
    for src, dst in zip(hbm, vmem):
        if dst.ndim == 3:
            for g in range(dst.shape[0]):
                chunks.append((src.at[layer, g], dst.at[g], dst.shape[2]))
            continue
        rows, cols = dst.shape
        for c0 in range(0, cols, STAGE_COLS):
            for r0 in range(0, rows, STAGE_ROWS):
                window = (pl.ds(r0, STAGE_ROWS), pl.ds(c0, STAGE_COLS))
                chunks.append((src.at[(layer,) + window], dst.at[window], STAGE_COLS))
    return chunks


def _load_weights(layer, hbm, vmem, stage, sem):
    chunks = _weight_chunks(layer, hbm, vmem)

    def copy(i):
        src, _, cols = chunks[i]
        slot = i % STAGE_SLOTS
        return pltpu.make_async_copy(src, stage.at[slot, :, pl.ds(0, cols)], sem.at[slot])

    for i in range(min(STAGE_SLOTS, len(chunks))):
        copy(i).start()
    for i, (_, dst, cols) in enumerate(chunks):
        copy(i).wait()
        dst[...] = stage[i % STAGE_SLOTS, :, 0:cols].astype(dst.dtype)
        if i + STAGE_SLOTS < len(chunks):
            copy(i + STAGE_SLOTS).start()


def _fused_kernel(xs_ref, ps_ref, ch_ref, ph_ref, xp_ref, pp_ref,
                  g_pre_ref, conv_w_ref, conv_b_ref, pool_scale_ref, g_post_ref,
                  w_in_hbm, w_grp_hbm, w_a_hbm, w_b_hbm, w_o_hbm, w_ple_hbm, w_pgate_hbm,
                  ys_ref, ncs_ref, nps_ref, yp_ref, ncp_ref, npp_ref,
                  w_in, w_grp, w_a, w_b, w_o, w_ple, w_pgate, stage, sem,
                  ubuf_s, xbuf_s, ubuf_p, xbuf_p, *, layer, sample_steps, prompt_tiles_per_seq):
    j = pl.program_id(0)
    hbm = (w_in_hbm, w_grp_hbm, w_a_hbm, w_b_hbm, w_o_hbm, w_ple_hbm, w_pgate_hbm)
    vmem = (w_in, w_grp, w_a, w_b, w_o, w_ple, w_pgate)

    @pl.when(j == 0)
    def _():
        _load_weights(layer, hbm, vmem, stage, sem)

    W = _Weights(g_pre_ref, conv_w_ref, conv_b_ref, pool_scale_ref, g_post_ref, *vmem)

    @pl.when(j < sample_steps)
    def _():
        _tile(xs_ref, ps_ref, ys_ref, ncs_ref, nps_ref, ubuf_s, xbuf_s, W,
              t=0, pos0=PAST_LEN, history=(ch_ref, ph_ref))

    @pl.when(j >= sample_steps)
    def _():
        t = lax.rem(j - sample_steps, prompt_tiles_per_seq)
        _tile(xp_ref, pp_ref, yp_ref, ncp_ref, npp_ref, ubuf_p, xbuf_p, W,
              t=t, pos0=t * xp_ref.shape[1], history=None)


def _vmem_bytes(shape, dtype):
    itemsize = jnp.dtype(dtype).itemsize
    sublanes = SUBLANES * 4 // itemsize
    *lead, rows, cols = shape
    n = itemsize * (-(-rows // sublanes) * sublanes) * (-(-cols // LANES) * LANES)
    for s in lead:
        n *= s
    return n


def _layer(layer, xs, ps, conv_hist, pool_hist, xp, pp, small, big):
    f32, bf16 = jnp.float32, jnp.bfloat16
    Bs, Ts, D = xs.shape
    Bp, Tp, _ = xp.shape
    Ss, Ls = SAMPLE_TILE
    Sp, Lp = PROMPT_TILE
    assert Ls == Ts and Bs % Ss == 0 and Sp == 1 and Tp % Lp == 0 and min(Ls, Lp) >= POOL_HIST
    sample_steps = Bs // Ss
    tiles_per_seq = Tp // Lp
    grid = (sample_steps + Bp * tiles_per_seq,)

    def sample_block(j):
        return jnp.minimum(j, sample_steps - 1)

    def prompt_block(j):
        q = jnp.maximum(j - sample_steps, 0)
        return q // tiles_per_seq, q % tiles_per_seq

    def sample_rows(width):
        return pl.BlockSpec((Ss, Ls, width), lambda j: (sample_block(j), 0, 0))

    def prompt_rows(width):
        return pl.BlockSpec((Sp, Lp, width), lambda j: prompt_block(j) + (0,))

    def sample_state(rows, depth_index):
        return pl.BlockSpec((None, Ss, rows, D), lambda j: (depth_index, sample_block(j), 0, 0))

    def prompt_state(rows):
        return pl.BlockSpec((None, Sp, rows, D), lambda j: (0, prompt_block(j)[0], 0, 0))

    def resident(arr):
        return pl.BlockSpec(arr.shape, lambda j: (0,) * arr.ndim, pipeline_mode=pl.Buffered(1))

    g_pre, conv_w, conv_b, pool_scale, g_post = small
    small_specs = [resident(g_pre),
                   pl.BlockSpec((None,) + conv_w.shape[1:], lambda j: (layer, 0, 0),
                                pipeline_mode=pl.Buffered(1)),
                   resident(conv_b), resident(pool_scale), resident(g_post)]
    in_specs = ([sample_rows(D), sample_rows(ps.shape[-1]), sample_state(CONV_W - 1, layer),
                 sample_state(POOL_HIST, layer), prompt_rows(D), prompt_rows(pp.shape[-1])]
                + small_specs + [pl.BlockSpec(memory_space=pl.ANY)] * len(big))
    out_specs = [sample_rows(D), sample_state(CONV_W - 1, 0), sample_state(POOL_HIST, 0),
                 prompt_rows(D), prompt_state(CONV_W - 1), prompt_state(POOL_HIST)]
    out_shape = [jax.ShapeDtypeStruct((Bs, Ts, D), f32),
                 jax.ShapeDtypeStruct((1, Bs, CONV_W - 1, D), f32),
                 jax.ShapeDtypeStruct((1, Bs, POOL_HIST, D), f32),
                 jax.ShapeDtypeStruct((Bp, Tp, D), f32),
                 jax.ShapeDtypeStruct((1, Bp, CONV_W - 1, D), f32),
                 jax.ShapeDtypeStruct((1, Bp, POOL_HIST, D), f32)]
    scratch = [(w.shape[1:], bf16) for w in big]
    scratch += [((STAGE_SLOTS, STAGE_ROWS, STAGE_COLS), f32)]
    buffers = [((Ss, CONV_PAD + Ls, D), f32), ((Ss, POOL_PAD + Ls, D), f32),
               ((Sp, CONV_PAD + Lp, D), f32), ((Sp, POOL_PAD + Lp, D), f32)]
    scratch_shapes = ([pltpu.VMEM(s, d) for s, d in scratch]
                      + [pltpu.SemaphoreType.DMA((STAGE_SLOTS,))]
                      + [pltpu.VMEM(s, d) for s, d in buffers])

    blocks = [((Ss, Ls, D), 2), ((Ss, Ls, ps.shape[-1]), 1), ((Ss, CONV_W - 1, D), 2),
              ((Ss, POOL_HIST, D), 2), ((Sp, Lp, D), 2), ((Sp, Lp, pp.shape[-1]), 1),
              ((Sp, CONV_W - 1, D), 1), ((Sp, POOL_HIST, D), 1)]
    vmem_estimate = (sum(_vmem_bytes(s, d) for s, d in scratch + buffers)
                     + sum(2 * n * _vmem_bytes(s, f32) for s, n in blocks)
                     + sum(_vmem_bytes(a.shape[-2:], f32) for a in small)
                     + 2 * LIVE_RESULTS_PER_CHAIN * _vmem_bytes((CHAIN_ROWS, D), f32))
    assert vmem_estimate <= V7X_VMEM_BYTES, vmem_estimate

    return pl.pallas_call(
        functools.partial(_fused_kernel, layer=layer, sample_steps=sample_steps,
                          prompt_tiles_per_seq=tiles_per_seq),
        grid=grid,
        in_specs=in_specs,
        out_specs=out_specs,
        out_shape=out_shape,
        scratch_shapes=scratch_shapes,
        compiler_params=pltpu.CompilerParams(
            dimension_semantics=("arbitrary",),
            vmem_limit_bytes=vmem_estimate),
    )(xs, ps, conv_hist, pool_hist, xp, pp, *small, *big)


def kernel(x_prompt, x_sample, p_prompt, p_sample, cache_conv, state_pool, g_pre, w_in,
           conv_w, conv_b, w_grp, pool_scale, w_a_out, w_b_out, w_o, g_post, w_ple, w_pgate):
    depth = w_in.shape[0]
    hp, hs = x_prompt, x_sample
    big = (w_in, w_grp, w_a_out, w_b_out, w_o, w_ple, w_pgate)
    conv_p, conv_s, pool_p, pool_s = [], [], [], []
    for i in range(depth):
        small = (g_pre[i][None, :], conv_w, conv_b[i][None, :], pool_scale[i][None, :],
                 g_post[i][None, :])
        hs, nc_s, np_s, hp, nc_p, np_p = _layer(i, hs, p_sample[i], cache_conv, state_pool,
                                                hp, p_prompt[i], small, big)
        conv_p.append(nc_p); pool_p.append(np_p)
        conv_s.append(nc_s); pool_s.append(np_s)

    def stacked(parts):
        return parts[0] if depth == 1 else jnp.concatenate(parts, axis=0)

    return (hp, hs, stacked(conv_p), stacked(conv_s), stacked(pool_p), stacked(pool_s))
```

```python
import functools
from typing import Any, NamedTuple

import jax
import jax.numpy as jnp
from jax import lax
from jax.experimental import pallas as pl
from jax.experimental.pallas import tpu as pltpu

D_MODEL = 1024
CONV_W = 3
POOL_WINDOWS = (2, 4, 8, 16)
N_POOL_GROUPS = len(POOL_WINDOWS)
POOL_GW = D_MODEL // N_POOL_GROUPS
POOL_HIST = max(POOL_WINDOWS) - 1
PAST_LEN = 4096
EPS = 1e-6

SEG_XA, SEG_BA, SEG_CA, SEG_ZA, SEG_XB, SEG_ZB, SEG_GA, SEG_GB = range(8)

CONV_PAD = 8
POOL_PAD = 16
assert CONV_PAD >= CONV_W - 1 and POOL_PAD >= POOL_HIST

SUBLANES, LANES = 8, 128
V7X_VMEM_BYTES = 64 * 1024 * 1024
LIVE_RESULTS_PER_CHAIN = 3

PROMPT_TILE = (1, 512)
SAMPLE_TILE = (8, 32)
CHAIN_ROWS = 256

STAGE_ROWS, STAGE_COLS, STAGE_SLOTS = D_MODEL, 512, 2


class _Unit(NamedTuple):
    name: str
    group: Any
    col0: int
    rows: int
    cols: int


def _stream_units(mats):
    D = D_MODEL

    def pieces(name, col0=0, ncols=None, group=None):
        shape = mats[name].shape[-2:]
        ncols = shape[1] if ncols is None else ncols
        step = min(ncols, STAGE_COLS)
        return [_Unit(name, group, c, shape[0], step) for c in range(col0, col0 + ncols, step)]

    def seg(s):
        return pieces("w_in", s * D, D)

    units = seg(SEG_XB) + seg(SEG_CA) + seg(SEG_XA) + seg(SEG_BA) + seg(SEG_ZA) + seg(SEG_ZB)
    for g in range(N_POOL_GROUPS):
        units += pieces("w_grp", group=g)
    units += pieces("w_a") + seg(SEG_GA) + pieces("w_b") + seg(SEG_GB)
    units += pieces("w_ple") + pieces("w_o") + pieces("w_pgate")
    return units


class _WeightStream:
    def __init__(self, layer, hbm, mats, stage, sem):
        self.layer, self.hbm, self.mats, self.stage, self.sem = layer, hbm, mats, stage, sem
        self.units = _stream_units(mats)
        self.index = {u[:3]: i for i, u in enumerate(self.units)}
        self.arrived = 0

    def _copy(self, i):
        u = self.units[i]
        lead = (self.layer,) if u.group is None else (self.layer, u.group)
        src = self.hbm[u.name].at[lead + (pl.ds(0, u.rows), pl.ds(u.col0, u.cols))]
        slot = i % STAGE_SLOTS
        return pltpu.make_async_copy(src, self.stage.at[slot, pl.ds(0, u.rows), pl.ds(0, u.cols)],
                                     self.sem.at[slot])

    def start(self):
        for i in range(min(STAGE_SLOTS, len(self.units))):
            self._copy(i).start()

    def _arrive(self):
        i = self.arrived
        u = self.units[i]
        self._copy(i).wait()
        dst = self.mats[u.name] if u.group is None else self.mats[u.name].at[u.group]
        dst[:, u.col0:u.col0 + u.cols] = (
            self.stage[i % STAGE_SLOTS, 0:u.rows, 0:u.cols].astype(dst.dtype))
        if i + STAGE_SLOTS < len(self.units):
            self._copy(i + STAGE_SLOTS).start()
        self.arrived += 1

    def need(self, name, group, col0):
        i = self.index[(name, group, col0)]
        assert i <= self.arrived, "weights are read in another order than they are streamed"
        while self.arrived <= min(i + 1, len(self.units) - 1):
            self._arrive()

    def finish(self):
        while self.arrived < len(self.units):
            self._arrive()


class _Weights:
    def __init__(self, g_pre, conv_w, conv_b, pool_scale, g_post, mats, stream=None):
        self.g_pre, self.conv_w, self.conv_b = g_pre, conv_w, conv_b
        self.pool_scale, self.g_post = pool_scale, g_post
        self.mats, self.stream = mats, stream

    def dot(self, lhs, name, *, group=None, col0=0, ncols=None):
        ref = self.mats[name] if group is None else self.mats[name].at[group]
        ncols = ref.shape[1] if ncols is None else ncols
        step = ncols if self.stream is None else min(ncols, STAGE_COLS)
        out = []
        for c in range(col0, col0 + ncols, step):
            if self.stream is not None:
                self.stream.need(name, group, c)
            out.append(jnp.dot(lhs, ref[:, c:c + step], preferred_element_type=jnp.float32))
        return out[0] if len(out) == 1 else jnp.concatenate(out, axis=-1)


def _rmsnorm(x, g):
    ms = jnp.mean(x * x, axis=-1, keepdims=True)
    return x * lax.rsqrt(ms + EPS) * g


def _shift_rows(v, k):
    return pltpu.roll(v, k, axis=v.ndim - 2)


def _trailing_sum(v, w):
    assert w & (w - 1) == 0
    k = 1
    while k < w:
        v = v + _shift_rows(v, k)
        k *= 2
    return v


_TAIL = "tail"
_STAY = "stay"


def _chain(s0, r0, Ss, R, pos0, x_ref, p_ref, y_ref, ubuf, xbuf, W, embed_first):
    D = x_ref.shape[-1]
    M = Ss * R
    f32, bf16 = jnp.float32, jnp.bfloat16
    streams = slice(s0, s0 + Ss)

    def embed():
        p = p_ref[streams, r0:r0 + R, :].reshape(M, p_ref.shape[-1])
        return W.dot(p.astype(bf16), "w_ple")

    if embed_first:
        pe = embed()
        yield _STAY
    x = x_ref[streams, r0:r0 + R, :].reshape(M, D)
    h = _rmsnorm(x, W.g_pre[...]).astype(bf16)
    yield

    def proj(seg):
        return W.dot(h, "w_in", col0=seg * D, ncols=D)

    xb = proj(SEG_XB).reshape(Ss, R, D)
    yield
    xbuf[streams, POOL_PAD + r0:POOL_PAD + r0 + R, :] = xb
    ca = proj(SEG_CA)
    yield
    u = (ca * proj(SEG_XA)).reshape(Ss, R, D)
    yield
    ubuf[streams, CONV_PAD + r0:CONV_PAD + r0 + R, :] = u

    pos = lax.broadcasted_iota(jnp.int32, (Ss, R, POOL_GW), 1) + (pos0 + r0)
    d = []
    for g, w in enumerate(POOL_WINDOWS):
        cols = slice(g * POOL_GW, (g + 1) * POOL_GW)
        xg = xb[:, :, cols]
        ext = jnp.concatenate([xbuf[streams, r0:r0 + POOL_PAD, cols], xg], axis=1)
        s = _trailing_sum(ext, w)[:, POOL_PAD:, :]
        cnt = jnp.minimum(pos + 1, w).astype(f32)
        d.append((s / cnt - xg).reshape(M, POOL_GW).astype(bf16))

    ext = jnp.concatenate([ubuf[streams, r0:r0 + CONV_PAD, :], u], axis=1)
    conv = W.conv_b[...] + W.conv_w[CONV_W - 1:CONV_W, :] * u
    for k in range(CONV_W - 1):
        shifted = _shift_rows(ext, CONV_W - 1 - k)[:, CONV_PAD:, :]
        conv = conv + W.conv_w[k:k + 1, :] * shifted
    ba = proj(SEG_BA)
    yield
    a = ((ba * conv.reshape(M, D)) * jax.nn.silu(proj(SEG_ZA))).astype(bf16)
    yield
    zb = proj(SEG_ZB)
    yield
    mixed = []
    for g in range(N_POOL_GROUPS):
        mixed.append(W.dot(d[g], "w_grp", group=g))
        yield
    mixed = jnp.concatenate(mixed, axis=-1) * W.pool_scale[...]
    b = (mixed * jax.nn.silu(zb)).astype(bf16)
    ya = W.dot(a, "w_a")
    yield
    m = jax.nn.sigmoid(proj(SEG_GA)) * ya
    yield
    yb = W.dot(b, "w_b")
    yield _TAIL
    m = (m + jax.nn.sigmoid(proj(SEG_GB)) * yb).astype(bf16)
    yield
    if not embed_first:
        pe = embed()
        yield

    n_parts = 1 if embed_first else 2
    parts = [(k * M // n_parts, (k + 1) * M // n_parts) for k in range(n_parts)]
    o = []
    for lo, hi in parts:
        o.append(W.dot(m[lo:hi], "w_o"))
        yield
    for (lo, hi), o_part in zip(parts, o):
        x1 = x[lo:hi] + _rmsnorm(o_part, W.g_post[...])
        gate = W.dot(x1.astype(bf16), "w_pgate")
        yield
        y = x1 + pe[lo:hi] * jax.nn.sigmoid(gate)
        if Ss > 1:
            y_ref[s0 + lo // R:s0 + hi // R, r0:r0 + R, :] = y.reshape((hi - lo) // R, R, D)
        else:
            y_ref[streams, r0 + lo:r0 + hi, :] = y.reshape(1, hi - lo, D)


def _run_interleaved(chains):
    done = object()
    prev = None
    for cur in chains:
        for marker in cur:
            if marker is _STAY:
                continue
            if prev is not None and next(prev, done) is done:
                prev = None
            if marker is _TAIL:
                break
        if prev is not None:
            for _ in prev:
                pass
        prev = cur
    for _ in prev:
        pass


def _chain_shape(S, L):
    if L >= CHAIN_ROWS:
        assert L % CHAIN_ROWS == 0
        return 1, CHAIN_ROWS
    assert CHAIN_ROWS % L == 0
    return min(S, CHAIN_ROWS // L), L


def _tile(x_ref, p_ref, y_ref, nc_ref, np_ref, ubuf, xbuf, W, *, t, pos0, history):
    S, L, _ = x_ref.shape
    Ss, R = _chain_shape(S, L)

    @pl.when(t == 0)
    def _():
        ubuf[:, 0:CONV_PAD, :] = jnp.zeros((S, CONV_PAD, ubuf.shape[-1]), ubuf.dtype)
        xbuf[:, 0:POOL_PAD, :] = jnp.zeros((S, POOL_PAD, xbuf.shape[-1]), xbuf.dtype)
        if history is not None:
            conv_hist_ref, pool_hist_ref = history
            ubuf[:, CONV_PAD - (CONV_W - 1):CONV_PAD, :] = conv_hist_ref[...]
            xbuf[:, POOL_PAD - POOL_HIST:POOL_PAD, :] = pool_hist_ref[...]

    starts = [(s0, r0) for s0 in range(0, S, Ss) for r0 in range(0, L, R)]
    assert W.stream is None or len(starts) == 1
    _run_interleaved(_chain(s0, r0, Ss, R, pos0, x_ref, p_ref, y_ref, ubuf, xbuf, W,
                            embed_first=i < len(starts) - 1)
                     for i, (s0, r0) in enumerate(starts))

    new_conv = ubuf[:, CONV_PAD + L - (CONV_W - 1):CONV_PAD + L, :]
    new_pool = xbuf[:, POOL_PAD + L - POOL_HIST:POOL_PAD + L, :]
    nc_ref[...] = new_conv
    np_ref[...] = new_pool
    ubuf[:, CONV_PAD - (CONV_W - 1):CONV_PAD, :] = new_conv
    xbuf[:, POOL_PAD - POOL_HIST:POOL_PAD, :] = new_pool


MATS = ("w_in", "w_grp", "w_a", "w_b", "w_o", "w_ple", "w_pgate")


def _fused_kernel(xs_ref, ps_ref, ch_ref, ph_ref, xp_ref, pp_ref,
                  g_pre_ref, conv_w_ref, conv_b_ref, pool_scale_ref, g_post_ref,
                  w_in_hbm, w_grp_hbm, w_a_hbm, w_b_hbm, w_o_hbm, w_ple_hbm, w_pgate_hbm,
                  ys_ref, ncs_ref, nps_ref, yp_ref, ncp_ref, npp_ref,
                  w_in, w_grp, w_a, w_b, w_o, w_ple, w_pgate, stage, sem,
                  ubuf_s, xbuf_s, ubuf_p, xbuf_p, *, layer, sample_steps, prompt_tiles_per_seq):
    j = pl.program_id(0)
    hbm = dict(zip(MATS, (w_in_hbm, w_grp_hbm, w_a_hbm, w_b_hbm, w_o_hbm, w_ple_hbm, w_pgate_hbm)))
    mats = dict(zip(MATS, (w_in, w_grp, w_a, w_b, w_o, w_ple, w_pgate)))
    vectors = (g_pre_ref, conv_w_ref, conv_b_ref, pool_scale_ref, g_post_ref)
    W = _Weights(*vectors, mats)

    def sample_tile(weights):
        _tile(xs_ref, ps_ref, ys_ref, ncs_ref, nps_ref, ubuf_s, xbuf_s, weights,
              t=0, pos0=PAST_LEN, history=(ch_ref, ph_ref))

    @pl.when(j == 0)
    def _():
        stream = _WeightStream(layer, hbm, mats, stage, sem)
        stream.start()
        sample_tile(_Weights(*vectors, mats, stream))
        stream.finish()

    @pl.when((j > 0) & (j < sample_steps))
    def _():
        sample_tile(W)

    @pl.when(j >= sample_steps)
    def _():
        t = lax.rem(j - sample_steps, prompt_tiles_per_seq)
        _tile(xp_ref, pp_ref, yp_ref, ncp_ref, npp_ref, ubuf_p, xbuf_p, W,
              t=t, pos0=t * xp_ref.shape[1], history=None)


def _vmem_bytes(shape, dtype):
    itemsize = jnp.dtype(dtype).itemsize
    sublanes = SUBLANES * 4 // itemsize
    *lead, rows, cols = shape
    n = itemsize * (-(-rows // sublanes) * sublanes) * (-(-cols // LANES) * LANES)
    for s in lead:
        n *= s
    return n


def _layer(layer, xs, ps, conv_hist, pool_hist, xp, pp, small, big):
    f32, bf16 = jnp.float32, jnp.bfloat16
    Bs, Ts, D = xs.shape
    Bp, Tp, _ = xp.shape
    Ss, Ls = SAMPLE_TILE
    Sp, Lp = PROMPT_TILE
    assert Ls == Ts and Bs % Ss == 0 and Sp == 1 and Tp % Lp == 0 and min(Ls, Lp) >= POOL_HIST
    sample_steps = Bs // Ss
    tiles_per_seq = Tp // Lp
    grid = (sample_steps + Bp * tiles_per_seq,)

    def sample_block(j):
        return jnp.minimum(j, sample_steps - 1)

    def prompt_block(j):
        q = jnp.maximum(j - sample_steps, 0)
        return q // tiles_per_seq, q % tiles_per_seq

    def sample_rows(width):
        return pl.BlockSpec((Ss, Ls, width), lambda j: (sample_block(j), 0, 0))

    def prompt_rows(width):
        return pl.BlockSpec((Sp, Lp, width), lambda j: prompt_block(j) + (0,))

    def sample_state(rows, depth_index):
        return pl.BlockSpec((None, Ss, rows, D), lambda j: (depth_index, sample_block(j), 0, 0))

    def prompt_state(rows):
        return pl.BlockSpec((None, Sp, rows, D), lambda j: (0, prompt_block(j)[0], 0, 0))

    def resident(arr):
        return pl.BlockSpec(arr.shape, lambda j: (0,) * arr.ndim, pipeline_mode=pl.Buffered(1))

    g_pre, conv_w, conv_b, pool_scale, g_post = small
    small_specs = [resident(g_pre),
                   pl.BlockSpec((None,) + conv_w.shape[1:], lambda j: (layer, 0, 0),
                                pipeline_mode=pl.Buffered(1)),
                   resident(conv_b), resident(pool_scale), resident(g_post)]
    in_specs = ([sample_rows(D), sample_rows(ps.shape[-1]), sample_state(CONV_W - 1, layer),
                 sample_state(POOL_HIST, layer), prompt_rows(D), prompt_rows(pp.shape[-1])]
                + small_specs + [pl.BlockSpec(memory_space=pl.ANY)] * len(big))
    out_specs = [sample_rows(D), sample_state(CONV_W - 1, 0), sample_state(POOL_HIST, 0),
                 prompt_rows(D), prompt_state(CONV_W - 1), prompt_state(POOL_HIST)]
    out_shape = [jax.ShapeDtypeStruct((Bs, Ts, D), f32),
                 jax.ShapeDtypeStruct((1, Bs, CONV_W - 1, D), f32),
                 jax.ShapeDtypeStruct((1, Bs, POOL_HIST, D), f32),
                 jax.ShapeDtypeStruct((Bp, Tp, D), f32),
                 jax.ShapeDtypeStruct((1, Bp, CONV_W - 1, D), f32),
                 jax.ShapeDtypeStruct((1, Bp, POOL_HIST, D), f32)]
    scratch = [(w.shape[1:], bf16) for w in big]
    scratch += [((STAGE_SLOTS, STAGE_ROWS, STAGE_COLS), f32)]
    buffers = [((Ss, CONV_PAD + Ls, D), f32), ((Ss, POOL_PAD + Ls, D), f32),
               ((Sp, CONV_PAD + Lp, D), f32), ((Sp, POOL_PAD + Lp, D), f32)]
    scratch_shapes = ([pltpu.VMEM(s, d) for s, d in scratch]
                      + [pltpu.SemaphoreType.DMA((STAGE_SLOTS,))]
                      + [pltpu.VMEM(s, d) for s, d in buffers])

    blocks = [((Ss, Ls, D), 2), ((Ss, Ls, ps.shape[-1]), 1), ((Ss, CONV_W - 1, D), 2),
              ((Ss, POOL_HIST, D), 2), ((Sp, Lp, D), 2), ((Sp, Lp, pp.shape[-1]), 1),
              ((Sp, CONV_W - 1, D), 1), ((Sp, POOL_HIST, D), 1)]
    vmem_estimate = (sum(_vmem_bytes(s, d) for s, d in scratch + buffers)
                     + sum(2 * n * _vmem_bytes(s, f32) for s, n in blocks)
                     + sum(_vmem_bytes(a.shape[-2:], f32) for a in small)
                     + 2 * LIVE_RESULTS_PER_CHAIN * _vmem_bytes((CHAIN_ROWS, D), f32))
    assert vmem_estimate <= V7X_VMEM_BYTES, vmem_estimate

    return pl.pallas_call(
        functools.partial(_fused_kernel, layer=layer, sample_steps=sample_steps,
                          prompt_tiles_per_seq=tiles_per_seq),
        grid=grid,
        in_specs=in_specs,
        out_specs=out_specs,
        out_shape=out_shape,
        scratch_shapes=scratch_shapes,
        compiler_params=pltpu.CompilerParams(
            dimension_semantics=("arbitrary",),
            vmem_limit_bytes=vmem_estimate),
    )(xs, ps, conv_hist, pool_hist, xp, pp, *small, *big)


def kernel(x_prompt, x_sample, p_prompt, p_sample, cache_conv, state_pool, g_pre, w_in,
           conv_w, conv_b, w_grp, pool_scale, w_a_out, w_b_out, w_o, g_post, w_ple, w_pgate):
    depth = w_in.shape[0]
    hp, hs = x_prompt, x_sample
    big = (w_in, w_grp, w_a_out, w_b_out, w_o, w_ple, w_pgate)
    conv_p, conv_s, pool_p, pool_s = [], [], [], []
    for i in range(depth):
        small = (g_pre[i][None, :], conv_w, conv_b[i][None, :], pool_scale[i][None, :],
                 g_post[i][None, :])
        hs, nc_s, np_s, hp, nc_p, np_p = _layer(i, hs, p_sample[i], cache_conv, state_pool,
                                                hp, p_prompt[i], small, big)
        conv_p.append(nc_p); pool_p.append(np_p)
        conv_s.append(nc_s); pool_s.append(np_s)

    def stacked(parts):
        return parts[0] if depth == 1 else jnp.concatenate(parts, axis=0)

    return (hp, hs, stacked(conv_p), stacked(conv_s), stacked(pool_p), stacked(pool_s))
```

```python
import functools
from typing import Any, NamedTuple

import jax
import jax.numpy as jnp
from jax import lax
from jax.experimental import pallas as pl
from jax.experimental.pallas import tpu as pltpu

D_MODEL = 1024
CONV_W = 3
POOL_WINDOWS = (2, 4, 8, 16)
N_POOL_GROUPS = len(POOL_WINDOWS)
POOL_GW = D_MODEL // N_POOL_GROUPS
POOL_HIST = max(POOL_WINDOWS) - 1
PAST_LEN = 4096
EPS = 1e-6

SEG_XA, SEG_BA, SEG_CA, SEG_ZA, SEG_XB, SEG_ZB, SEG_GA, SEG_GB = range(8)

CONV_PAD = 8
POOL_PAD = 16
assert CONV_PAD >= CONV_W - 1 and POOL_PAD >= POOL_HIST

SUBLANES, LANES = 8, 128
V7X_VMEM_BYTES = 64 * 1024 * 1024
LIVE_RESULTS_PER_CHAIN = 3

PROMPT_TILE = (1, 512)
SAMPLE_TILE = (8, 32)
CHAIN_ROWS = 256

STAGE_ROWS, STAGE_COLS, STAGE_SLOTS = 256, 1024, 4

MATS = ("w_in", "w_grp", "w_a", "w_b", "w_o", "w_ple", "w_pgate")

STATE_TIME_MAJOR = (0, 2, 1, 3)


class _Weights(NamedTuple):
    g_pre: Any
    conv_w: Any
    conv_b: Any
    pool_scale: Any
    g_post: Any
    mats: Any

    def dot(self, lhs, name, *, group=None, col0=0, ncols=None):
        ref = self.mats[name] if group is None else self.mats[name].at[group]
        ncols = ref.shape[1] if ncols is None else ncols
        return jnp.dot(lhs, ref[:, col0:col0 + ncols], preferred_element_type=jnp.float32)


def _weight_chunks(layer, hbm, mats):
    chunks = []
    for name in MATS:
        src, dst = hbm[name], mats[name]
        if dst.ndim == 3:
            for g in range(dst.shape[0]):
                chunks.append((src.at[layer, g], dst.at[g], dst.shape[2]))
            continue
        rows, cols = dst.shape
        for c0 in range(0, cols, STAGE_COLS):
            for r0 in range(0, rows, STAGE_ROWS):
                window = (pl.ds(r0, STAGE_ROWS), pl.ds(c0, STAGE_COLS))
                chunks.append((src.at[(layer,) + window], dst.at[window], STAGE_COLS))
    return chunks


def _load_weights(layer, hbm, mats, stage, sem):
    chunks = _weight_chunks(layer, hbm, mats)

    def copy(i):
        src, _, cols = chunks[i]
        slot = i % STAGE_SLOTS
        return pltpu.make_async_copy(src, stage.at[slot, :, pl.ds(0, cols)], sem.at[slot])

    for i in range(min(STAGE_SLOTS, len(chunks))):
        copy(i).start()
    for i, (_, dst, cols) in enumerate(chunks):
        copy(i).wait()
        dst[...] = stage[i % STAGE_SLOTS, :, 0:cols].astype(dst.dtype)
        if i + STAGE_SLOTS < len(chunks):
            copy(i + STAGE_SLOTS).start()


def _rmsnorm(x, g):
    ms = jnp.mean(x * x, axis=-1, keepdims=True)
    return x * lax.rsqrt(ms + EPS) * g


def _shift_rows(v, k):
    return pltpu.roll(v, k, axis=v.ndim - 2)


def _trailing_sum(v, w):
    assert w & (w - 1) == 0
    k = 1
    while k < w:
        v = v + _shift_rows(v, k)
        k *= 2
    return v


_TAIL = "tail"
_STAY = "stay"


def _chain(s0, r0, Ss, R, pos0, x_ref, p_ref, y_ref, ubuf, xbuf, W, embed_first):
    D = x_ref.shape[-1]
    M = Ss * R
    f32, bf16 = jnp.float32, jnp.bfloat16
    streams = slice(s0, s0 + Ss)

    def embed():
        p = p_ref[streams, r0:r0 + R, :].reshape(M, p_ref.shape[-1])
        return W.dot(p.astype(bf16), "w_ple")

    if embed_first:
        pe = embed()
        yield _STAY
    x = x_ref[streams, r0:r0 + R, :].reshape(M, D)
    h = _rmsnorm(x, W.g_pre[...]).astype(bf16)
    yield

    def proj(seg):
        return W.dot(h, "w_in", col0=seg * D, ncols=D)

    xb = proj(SEG_XB).reshape(Ss, R, D)
    yield
    xbuf[streams, POOL_PAD + r0:POOL_PAD + r0 + R, :] = xb
    ca = proj(SEG_CA)
    yield
    u = (ca * proj(SEG_XA)).reshape(Ss, R, D)
    yield
    ubuf[streams, CONV_PAD + r0:CONV_PAD + r0 + R, :] = u

    pos = lax.broadcasted_iota(jnp.int32, (Ss, R, POOL_GW), 1) + (pos0 + r0)
    d = []
    for g, w in enumerate(POOL_WINDOWS):
        cols = slice(g * POOL_GW, (g + 1) * POOL_GW)
        xg = xb[:, :, cols]
        ext = jnp.concatenate([xbuf[streams, r0:r0 + POOL_PAD, cols], xg], axis=1)
        s = _trailing_sum(ext, w)[:, POOL_PAD:, :]
        cnt = jnp.minimum(pos + 1, w).astype(f32)
        d.append((s / cnt - xg).reshape(M, POOL_GW).astype(bf16))

    ext = jnp.concatenate([ubuf[streams, r0:r0 + CONV_PAD, :], u], axis=1)
    conv = W.conv_b[...] + W.conv_w[CONV_W - 1:CONV_W, :] * u
    for k in range(CONV_W - 1):
        shifted = _shift_rows(ext, CONV_W - 1 - k)[:, CONV_PAD:, :]
        conv = conv + W.conv_w[k:k + 1, :] * shifted
    ba = proj(SEG_BA)
    yield
    a = ((ba * conv.reshape(M, D)) * jax.nn.silu(proj(SEG_ZA))).astype(bf16)
    yield
    zb = proj(SEG_ZB)
    yield
    mixed = []
    for g in range(N_POOL_GROUPS):
        mixed.append(W.dot(d[g], "w_grp", group=g))
        yield
    mixed = jnp.concatenate(mixed, axis=-1) * W.pool_scale[...]
    b = (mixed * jax.nn.silu(zb)).astype(bf16)
    ya = W.dot(a, "w_a")
    yield
    m = jax.nn.sigmoid(proj(SEG_GA)) * ya
    yield
    yb = W.dot(b, "w_b")
    yield _TAIL
    m = (m + jax.nn.sigmoid(proj(SEG_GB)) * yb).astype(bf16)
    yield
    if not embed_first:
        pe = embed()
        yield

    n_parts = 1 if embed_first else 2
    parts = [(k * M // n_parts, (k + 1) * M // n_parts) for k in range(n_parts)]
    o = []
    for lo, hi in parts:
        o.append(W.dot(m[lo:hi], "w_o"))
        yield
    for (lo, hi), o_part in zip(parts, o):
        x1 = x[lo:hi] + _rmsnorm(o_part, W.g_post[...])
        gate = W.dot(x1.astype(bf16), "w_pgate")
        yield
        y = x1 + pe[lo:hi] * jax.nn.sigmoid(gate)
        if Ss > 1:
            y_ref[s0 + lo // R:s0 + hi // R, r0:r0 + R, :] = y.reshape((hi - lo) // R, R, D)
        else:
            y_ref[streams, r0 + lo:r0 + hi, :] = y.reshape(1, hi - lo, D)


def _run_interleaved(chains):
    done = object()
    prev = None
    for cur in chains:
        for marker in cur:
            if marker is _STAY:
                continue
            if prev is not None and next(prev, done) is done:
                prev = None
            if marker is _TAIL:
                break
        if prev is not None:
            for _ in prev:
                pass
        prev = cur
    for _ in prev:
        pass


def _chain_shape(S, L):
    if L >= CHAIN_ROWS:
        assert L % CHAIN_ROWS == 0
        return 1, CHAIN_ROWS
    assert CHAIN_ROWS % L == 0
    return min(S, CHAIN_ROWS // L), L


def _tile(x_ref, p_ref, y_ref, ubuf, xbuf, W, *, t, pos0, load_history):
    S, L, _ = x_ref.shape
    Ss, R = _chain_shape(S, L)

    @pl.when(t == 0)
    def _():
        ubuf[:, 0:CONV_PAD, :] = jnp.zeros((S, CONV_PAD, ubuf.shape[-1]), ubuf.dtype)
        xbuf[:, 0:POOL_PAD, :] = jnp.zeros((S, POOL_PAD, xbuf.shape[-1]), xbuf.dtype)
        if load_history is not None:
            load_history()

    starts = [(s0, r0) for s0 in range(0, S, Ss) for r0 in range(0, L, R)]
    _run_interleaved(_chain(s0, r0, Ss, R, pos0, x_ref, p_ref, y_ref, ubuf, xbuf, W,
                            embed_first=i < len(starts) - 1)
                     for i, (s0, r0) in enumerate(starts))

    ubuf[:, CONV_PAD - (CONV_W - 1):CONV_PAD, :] = ubuf[:, CONV_PAD + L - (CONV_W - 1):CONV_PAD + L, :]
    xbuf[:, POOL_PAD - POOL_HIST:POOL_PAD, :] = xbuf[:, POOL_PAD + L - POOL_HIST:POOL_PAD + L, :]


def _fused_kernel(xs_ref, ps_ref, ch_ref, ph_ref, xp_ref, pp_ref,
                  g_pre_ref, conv_w_ref, conv_b_ref, pool_scale_ref, g_post_ref,
                  w_in_hbm, w_grp_hbm, w_a_hbm, w_b_hbm, w_o_hbm, w_ple_hbm, w_pgate_hbm,
                  ys_ref, ncs_ref, nps_ref, yp_ref, ncp_ref, npp_ref,
                  w_in, w_grp, w_a, w_b, w_o, w_ple, w_pgate, stage, sem,
                  ubuf_s, xbuf_s, ubuf_p, xbuf_p, *, layer, sample_steps, prompt_tiles_per_seq):
    j = pl.program_id(0)
    hbm = dict(zip(MATS, (w_in_hbm, w_grp_hbm, w_a_hbm, w_b_hbm, w_o_hbm, w_ple_hbm, w_pgate_hbm)))
    mats = dict(zip(MATS, (w_in, w_grp, w_a, w_b, w_o, w_ple, w_pgate)))
    W = _Weights(g_pre_ref, conv_w_ref, conv_b_ref, pool_scale_ref, g_post_ref, mats)
    Ls, Lp = xs_ref.shape[1], xp_ref.shape[1]

    @pl.when(j == 0)
    def _():
        _load_weights(layer, hbm, mats, stage, sem)

    def sample_history():
        ubuf_s[:, CONV_PAD - (CONV_W - 1):CONV_PAD, :] = ch_ref[...]
        for k in range(POOL_HIST):
            xbuf_s[:, POOL_PAD - POOL_HIST + k, :] = ph_ref[k]

    @pl.when(j < sample_steps)
    def _():
        _tile(xs_ref, ps_ref, ys_ref, ubuf_s, xbuf_s, W, t=0, pos0=PAST_LEN,
              load_history=sample_history)
        ncs_ref[...] = ubuf_s[:, CONV_PAD + Ls - (CONV_W - 1):CONV_PAD + Ls, :]
        for k in range(POOL_HIST):
            nps_ref[k] = xbuf_s[:, POOL_PAD + Ls - POOL_HIST + k, :]

    @pl.when(j == sample_steps)
    def _():
        npp_ref[...] = jnp.zeros(npp_ref.shape, npp_ref.dtype)

    @pl.when(j >= sample_steps)
    def _():
        q = j - sample_steps
        b, t = q // prompt_tiles_per_seq, lax.rem(q, prompt_tiles_per_seq)
        _tile(xp_ref, pp_ref, yp_ref, ubuf_p, xbuf_p, W, t=t, pos0=t * Lp, load_history=None)
        ncp_ref[...] = ubuf_p[:, CONV_PAD + Lp - (CONV_W - 1):CONV_PAD + Lp, :]

        @pl.when(t == prompt_tiles_per_seq - 1)
        def _():
            row_shape = npp_ref.shape[1:]
            mine = lax.broadcasted_iota(jnp.int32, row_shape, 0) == b
            for k in range(POOL_HIST):
                row = xbuf_p[0, POOL_PAD + Lp - POOL_HIST + k:POOL_PAD + Lp - POOL_HIST + k + 1, :]
                npp_ref[k] = jnp.where(mine, jnp.broadcast_to(row, row_shape), npp_ref[k])


def _vmem_bytes(shape, dtype):
    itemsize = jnp.dtype(dtype).itemsize
    sublanes = SUBLANES * 4 // itemsize
    *lead, rows, cols = shape
    n = itemsize * (-(-rows // sublanes) * sublanes) * (-(-cols // LANES) * LANES)
    for s in lead:
        n *= s
    return n


def _layer(layer, xs, ps, conv_hist, pool_hist, xp, pp, small, big):
    f32, bf16 = jnp.float32, jnp.bfloat16
    Bs, Ts, D = xs.shape
    Bp, Tp, _ = xp.shape
    Ss, Ls = SAMPLE_TILE
    Sp, Lp = PROMPT_TILE
    assert Ls == Ts and Bs % Ss == 0 and Sp == 1 and Tp % Lp == 0 and min(Ls, Lp) >= POOL_HIST
    sample_steps = Bs // Ss
    tiles_per_seq = Tp // Lp
    grid = (sample_steps + Bp * tiles_per_seq,)

    def sample_block(j):
        return jnp.minimum(j, sample_steps - 1)

    def prompt_block(j):
        q = jnp.maximum(j - sample_steps, 0)
        return q // tiles_per_seq, q % tiles_per_seq

    def sample_rows(width):
        return pl.BlockSpec((Ss, Ls, width), lambda j: (sample_block(j), 0, 0))

    def prompt_rows(width):
        return pl.BlockSpec((Sp, Lp, width), lambda j: prompt_block(j) + (0,))

    def sample_conv(depth_index):
        return pl.BlockSpec((None, Ss, CONV_W - 1, D),
                            lambda j: (depth_index, sample_block(j), 0, 0))

    def sample_pool(depth_index):
        return pl.BlockSpec((None, POOL_HIST, Ss, D),
                            lambda j: (depth_index, 0, sample_block(j), 0))

    prompt_conv = pl.BlockSpec((None, Sp, CONV_W - 1, D), lambda j: (0, prompt_block(j)[0], 0, 0))
    prompt_pool = pl.BlockSpec((None, POOL_HIST, Bp, D), lambda j: (0, 0, 0, 0))

    def resident(arr):
        return pl.BlockSpec(arr.shape, lambda j: (0,) * arr.ndim, pipeline_mode=pl.Buffered(1))

    g_pre, conv_w, conv_b, pool_scale, g_post = small
    small_specs = [resident(g_pre),
                   pl.BlockSpec((None,) + conv_w.shape[1:], lambda j: (layer, 0, 0),
                                pipeline_mode=pl.Buffered(1)),
                   resident(conv_b), resident(pool_scale), resident(g_post)]
    in_specs = ([sample_rows(D), sample_rows(ps.shape[-1]), sample_conv(layer), sample_pool(layer),
                 prompt_rows(D), prompt_rows(pp.shape[-1])]
                + small_specs + [pl.BlockSpec(memory_space=pl.ANY)] * len(big))
    out_specs = [sample_rows(D), sample_conv(0), sample_pool(0),
                 prompt_rows(D), prompt_conv, prompt_pool]
    out_shape = [jax.ShapeDtypeStruct((Bs, Ts, D), f32),
                 jax.ShapeDtypeStruct((1, Bs, CONV_W - 1, D), f32),
                 jax.ShapeDtypeStruct((1, POOL_HIST, Bs, D), f32),
                 jax.ShapeDtypeStruct((Bp, Tp, D), f32),
                 jax.ShapeDtypeStruct((1, Bp, CONV_W - 1, D), f32),
                 jax.ShapeDtypeStruct((1, POOL_HIST, Bp, D), f32)]
    scratch = [(w.shape[1:], bf16) for w in big]
    scratch += [((STAGE_SLOTS, STAGE_ROWS, STAGE_COLS), f32)]
    buffers = [((Ss, CONV_PAD + Ls, D), f32), ((Ss, POOL_PAD + Ls, D), f32),
               ((Sp, CONV_PAD + Lp, D), f32), ((Sp, POOL_PAD + Lp, D), f32)]
    scratch_shapes = ([pltpu.VMEM(s, d) for s, d in scratch]
                      + [pltpu.SemaphoreType.DMA((STAGE_SLOTS,))]
                      + [pltpu.VMEM(s, d) for s, d in buffers])

    blocks = [((Ss, Ls, D), 2), ((Ss, Ls, ps.shape[-1]), 1), ((Ss, CONV_W - 1, D), 2),
              ((POOL_HIST, Ss, D), 2), ((Sp, Lp, D), 2), ((Sp, Lp, pp.shape[-1]), 1),
              ((Sp, CONV_W - 1, D), 1), ((POOL_HIST, Bp, D), 1)]
    vmem_estimate = (sum(_vmem_bytes(s, d) for s, d in scratch + buffers)
                     + sum(2 * n * _vmem_bytes(s, f32) for s, n in blocks)
                     + sum(_vmem_bytes(a.shape[-2:], f32) for a in small)
                     + 2 * LIVE_RESULTS_PER_CHAIN * _vmem_bytes((CHAIN_ROWS, D), f32))
    assert vmem_estimate <= V7X_VMEM_BYTES, vmem_estimate

    return pl.pallas_call(
        functools.partial(_fused_kernel, layer=layer, sample_steps=sample_steps,
                          prompt_tiles_per_seq=tiles_per_seq),
        grid=grid,
        in_specs=in_specs,
        out_specs=out_specs,
        out_shape=out_shape,
        scratch_shapes=scratch_shapes,
        compiler_params=pltpu.CompilerParams(
            dimension_semantics=("arbitrary",),
            vmem_limit_bytes=vmem_estimate),
    )(xs, ps, conv_hist, pool_hist, xp, pp, *small, *big)


def kernel(x_prompt, x_sample, p_prompt, p_sample, cache_conv, state_pool, g_pre, w_in,
           conv_w, conv_b, w_grp, pool_scale, w_a_out, w_b_out, w_o, g_post, w_ple, w_pgate):
    depth = w_in.shape[0]
    hp, hs = x_prompt, x_sample
    big = (w_in, w_grp, w_a_out, w_b_out, w_o, w_ple, w_pgate)
    pool_hist = jnp.transpose(state_pool, STATE_TIME_MAJOR)
    conv_p, conv_s, pool_p, pool_s = [], [], [], []
    for i in range(depth):
        small = (g_pre[i][None, :], conv_w, conv_b[i][None, :], pool_scale[i][None, :],
                 g_post[i][None, :])
        hs, nc_s, np_s, hp, nc_p, np_p = _layer(i, hs, p_sample[i], cache_conv, pool_hist,
                                                hp, p_prompt[i], small, big)
        conv_p.append(nc_p); pool_p.append(jnp.transpose(np_p, STATE_TIME_MAJOR))
        conv_s.append(nc_s); pool_s.append(jnp.transpose(np_s, STATE_TIME_MAJOR))

    def stacked(parts):
        return parts[0] if depth == 1 else jnp.concatenate(parts, axis=0)

    return (hp, hs, stacked(conv_p), stacked(conv_s), stacked(pool_p), stacked(pool_s))
```

```python
import functools
from typing import Any, NamedTuple

import jax
import jax.numpy as jnp
from jax import lax
from jax.experimental import pallas as pl
from jax.experimental.pallas import tpu as pltpu

D_MODEL = 1024
CONV_W = 3
POOL_WINDOWS = (2, 4, 8, 16)
N_POOL_GROUPS = len(POOL_WINDOWS)
POOL_GW = D_MODEL // N_POOL_GROUPS
POOL_HIST = max(POOL_WINDOWS) - 1
PAST_LEN = 4096
EPS = 1e-6

SEG_XA, SEG_BA, SEG_CA, SEG_ZA, SEG_XB, SEG_ZB, SEG_GA, SEG_GB = range(8)

CONV_PAD = 8
POOL_PAD = 16
assert CONV_PAD >= CONV_W - 1 and POOL_PAD >= POOL_HIST

SUBLANES, LANES = 8, 128
V7X_VMEM_BYTES = 64 * 1024 * 1024
LIVE_RESULTS_PER_CHAIN = 3

PROMPT_TILE = (1, 512)
SAMPLE_TILE = (8, 32)
CHAIN_ROWS = 256

STAGE_ROWS, STAGE_COLS, STAGE_SLOTS = 256, 1024, 4

MATS = ("w_in", "w_grp", "w_a", "w_b", "w_o", "w_ple", "w_pgate")

STATE_TIME_MAJOR = (0, 2, 1, 3)


class _Weights(NamedTuple):
    g_pre: Any
    conv_w: Any
    conv_b: Any
    pool_scale: Any
    g_post: Any
    mats: Any

    def dot(self, lhs, name, *, group=None, col0=0, ncols=None):
        ref = self.mats[name] if group is None else self.mats[name].at[group]
        ncols = ref.shape[1] if ncols is None else ncols
        return jnp.dot(lhs, ref[:, col0:col0 + ncols], preferred_element_type=jnp.float32)


def _weight_chunks(layer, hbm, mats):
    chunks = []
    for name in MATS:
        src, dst = hbm[name], mats[name]
        if dst.ndim == 3:
            for g in range(dst.shape[0]):
                chunks.append((src.at[layer, g], dst.at[g], dst.shape[2]))
            continue
        rows, cols = dst.shape
        for c0 in range(0, cols, STAGE_COLS):
            for r0 in range(0, rows, STAGE_ROWS):
                window = (pl.ds(r0, STAGE_ROWS), pl.ds(c0, STAGE_COLS))
                chunks.append((src.at[(layer,) + window], dst.at[window], STAGE_COLS))
    return chunks


def _load_weights(layer, hbm, mats, stage, sem):
    chunks = _weight_chunks(layer, hbm, mats)

    def copy(i):
        src, _, cols = chunks[i]
        slot = i % STAGE_SLOTS
        return pltpu.make_async_copy(src, stage.at[slot, :, pl.ds(0, cols)], sem.at[slot])

    for i in range(min(STAGE_SLOTS, len(chunks))):
        copy(i).start()
    for i, (_, dst, cols) in enumerate(chunks):
        copy(i).wait()
        dst[...] = stage[i % STAGE_SLOTS, :, 0:cols].astype(dst.dtype)
        if i + STAGE_SLOTS < len(chunks):
            copy(i + STAGE_SLOTS).start()


def _rmsnorm(x, g):
    ms = jnp.mean(x * x, axis=-1, keepdims=True)
    return x * lax.rsqrt(ms + EPS) * g


def _sigmoid(x):
    return 0.5 * jnp.tanh(0.5 * x) + 0.5


def _silu(x):
    half = 0.5 * x
    return half * jnp.tanh(half) + half


def _shift_rows(v, k):
    return pltpu.roll(v, k, axis=v.ndim - 2)


def _trailing_sum(v, w):
    assert w & (w - 1) == 0
    k = 1
    while k < w:
        v = v + _shift_rows(v, k)
        k *= 2
    return v


_TAIL = "tail"
_STAY = "stay"


def _chain(s0, r0, Ss, R, pos0, x_ref, p_ref, y_ref, ubuf, xbuf, W, embed_first):
    D = x_ref.shape[-1]
    M = Ss * R
    f32, bf16 = jnp.float32, jnp.bfloat16
    streams = slice(s0, s0 + Ss)

    def embed():
        p = p_ref[streams, r0:r0 + R, :].reshape(M, p_ref.shape[-1])
        return W.dot(p.astype(bf16), "w_ple")

    if embed_first:
        pe = embed()
        yield _STAY
    x = x_ref[streams, r0:r0 + R, :].reshape(M, D)
    h = _rmsnorm(x, W.g_pre[...]).astype(bf16)
    yield

    def proj(seg):
        return W.dot(h, "w_in", col0=seg * D, ncols=D)

    xb = proj(SEG_XB).reshape(Ss, R, D)
    yield
    xbuf[streams, POOL_PAD + r0:POOL_PAD + r0 + R, :] = xb
    ca = proj(SEG_CA)
    yield
    u = (ca * proj(SEG_XA)).reshape(Ss, R, D)
    yield
    ubuf[streams, CONV_PAD + r0:CONV_PAD + r0 + R, :] = u

    pos = lax.broadcasted_iota(jnp.int32, (Ss, R, POOL_GW), 1) + (pos0 + r0)
    d = []
    for g, w in enumerate(POOL_WINDOWS):
        cols = slice(g * POOL_GW, (g + 1) * POOL_GW)
        xg = xb[:, :, cols]
        ext = jnp.concatenate([xbuf[streams, r0:r0 + POOL_PAD, cols], xg], axis=1)
        s = _trailing_sum(ext, w)[:, POOL_PAD:, :]
        cnt = jnp.minimum(pos + 1, w).astype(f32)
        d.append((s / cnt - xg).reshape(M, POOL_GW).astype(bf16))

    ext = jnp.concatenate([ubuf[streams, r0:r0 + CONV_PAD, :], u], axis=1)
    conv = W.conv_b[...] + W.conv_w[CONV_W - 1:CONV_W, :] * u
    for k in range(CONV_W - 1):
        shifted = _shift_rows(ext, CONV_W - 1 - k)[:, CONV_PAD:, :]
        conv = conv + W.conv_w[k:k + 1, :] * shifted
    ba = proj(SEG_BA)
    yield
    a = ((ba * conv.reshape(M, D)) * _silu(proj(SEG_ZA))).astype(bf16)
    yield
    zb = proj(SEG_ZB)
    yield
    mixed = []
    for g in range(N_POOL_GROUPS):
        mixed.append(W.dot(d[g], "w_grp", group=g))
        yield
    mixed = jnp.concatenate(mixed, axis=-1) * W.pool_scale[...]
    b = (mixed * _silu(zb)).astype(bf16)
    ya = W.dot(a, "w_a")
    yield
    m = _sigmoid(proj(SEG_GA)) * ya
    yield
    yb = W.dot(b, "w_b")
    yield _TAIL
    m = (m + _sigmoid(proj(SEG_GB)) * yb).astype(bf16)
    yield
    if not embed_first:
        pe = embed()
        yield

    n_parts = 1 if embed_first else 2
    parts = [(k * M // n_parts, (k + 1) * M // n_parts) for k in range(n_parts)]
    o = []
    for lo, hi in parts:
        o.append(W.dot(m[lo:hi], "w_o"))
        yield
    for (lo, hi), o_part in zip(parts, o):
        x1 = x[lo:hi] + _rmsnorm(o_part, W.g_post[...])
        gate = W.dot(x1.astype(bf16), "w_pgate")
        yield
        y = x1 + pe[lo:hi] * _sigmoid(gate)
        if Ss > 1:
            y_ref[s0 + lo // R:s0 + hi // R, r0:r0 + R, :] = y.reshape((hi - lo) // R, R, D)
        else:
            y_ref[streams, r0 + lo:r0 + hi, :] = y.reshape(1, hi - lo, D)


def _run_interleaved(chains):
    done = object()
    prev = None
    for cur in chains:
        for marker in cur:
            if marker is _STAY:
                continue
            if prev is not None and next(prev, done) is done:
                prev = None
            if marker is _TAIL:
                break
        if prev is not None:
            for _ in prev:
                pass
        prev = cur
    for _ in prev:
        pass


def _chain_shape(S, L):
    if L >= CHAIN_ROWS:
        assert L % CHAIN_ROWS == 0
        return 1, CHAIN_ROWS
    assert CHAIN_ROWS % L == 0
    return min(S, CHAIN_ROWS // L), L


def _tile(x_ref, p_ref, y_ref, ubuf, xbuf, W, *, t, pos0, load_history):
    S, L, _ = x_ref.shape
    Ss, R = _chain_shape(S, L)

    @pl.when(t == 0)
    def _():
        ubuf[:, 0:CONV_PAD, :] = jnp.zeros((S, CONV_PAD, ubuf.shape[-1]), ubuf.dtype)
        xbuf[:, 0:POOL_PAD, :] = jnp.zeros((S, POOL_PAD, xbuf.shape[-1]), xbuf.dtype)
        if load_history is not None:
            load_history()

    starts = [(s0, r0) for s0 in range(0, S, Ss) for r0 in range(0, L, R)]
    _run_interleaved(_chain(s0, r0, Ss, R, pos0, x_ref, p_ref, y_ref, ubuf, xbuf, W,
                            embed_first=i < len(starts) - 1)
                     for i, (s0, r0) in enumerate(starts))

    ubuf[:, CONV_PAD - (CONV_W - 1):CONV_PAD, :] = ubuf[:, CONV_PAD + L - (CONV_W - 1):CONV_PAD + L, :]
    xbuf[:, POOL_PAD - POOL_HIST:POOL_PAD, :] = xbuf[:, POOL_PAD + L - POOL_HIST:POOL_PAD + L, :]


def _fused_kernel(xs_ref, ps_ref, ch_ref, ph_ref, xp_ref, pp_ref,
                  g_pre_ref, conv_w_ref, conv_b_ref, pool_scale_ref, g_post_ref,
                  w_in_hbm, w_grp_hbm, w_a_hbm, w_b_hbm, w_o_hbm, w_ple_hbm, w_pgate_hbm,
                  ys_ref, ncs_ref, nps_ref, yp_ref, ncp_ref, npp_ref,
                  w_in, w_grp, w_a, w_b, w_o, w_ple, w_pgate, stage, sem,
                  ubuf_s, xbuf_s, ubuf_p, xbuf_p, *, layer, sample_steps, prompt_tiles_per_seq):
    j = pl.program_id(0)
    hbm = dict(zip(MATS, (w_in_hbm, w_grp_hbm, w_a_hbm, w_b_hbm, w_o_hbm, w_ple_hbm, w_pgate_hbm)))
    mats = dict(zip(MATS, (w_in, w_grp, w_a, w_b, w_o, w_ple, w_pgate)))
    W = _Weights(g_pre_ref, conv_w_ref, conv_b_ref, pool_scale_ref, g_post_ref, mats)
    Ls, Lp = xs_ref.shape[1], xp_ref.shape[1]

    @pl.when(j == 0)
    def _():
        _load_weights(layer, hbm, mats, stage, sem)

    def sample_history():
        ubuf_s[:, CONV_PAD - (CONV_W - 1):CONV_PAD, :] = ch_ref[...]
        for k in range(POOL_HIST):
            xbuf_s[:, POOL_PAD - POOL_HIST + k, :] = ph_ref[k]

    @pl.when(j < sample_steps)
    def _():
        _tile(xs_ref, ps_ref, ys_ref, ubuf_s, xbuf_s, W, t=0, pos0=PAST_LEN,
              load_history=sample_history)
        ncs_ref[...] = ubuf_s[:, CONV_PAD + Ls - (CONV_W - 1):CONV_PAD + Ls, :]
        for k in range(POOL_HIST):
            nps_ref[k] = xbuf_s[:, POOL_PAD + Ls - POOL_HIST + k, :]

    @pl.when(j == sample_steps)
    def _():
        npp_ref[...] = jnp.zeros(npp_ref.shape, npp_ref.dtype)

    @pl.when(j >= sample_steps)
    def _():
        q = j - sample_steps
        b, t = q // prompt_tiles_per_seq, lax.rem(q, prompt_tiles_per_seq)
        _tile(xp_ref, pp_ref, yp_ref, ubuf_p, xbuf_p, W, t=t, pos0=t * Lp, load_history=None)
        ncp_ref[...] = ubuf_p[:, CONV_PAD + Lp - (CONV_W - 1):CONV_PAD + Lp, :]

        @pl.when(t == prompt_tiles_per_seq - 1)
        def _():
            row_shape = npp_ref.shape[1:]
            mine = lax.broadcasted_iota(jnp.int32, row_shape, 0) == b
            for k in range(POOL_HIST):
                row = xbuf_p[0, POOL_PAD + Lp - POOL_HIST + k:POOL_PAD + Lp - POOL_HIST + k + 1, :]
                npp_ref[k] = jnp.where(mine, jnp.broadcast_to(row, row_shape), npp_ref[k])


def _vmem_bytes(shape, dtype):
    itemsize = jnp.dtype(dtype).itemsize
    sublanes = SUBLANES * 4 // itemsize
    *lead, rows, cols = shape
    n = itemsize * (-(-rows // sublanes) * sublanes) * (-(-cols // LANES) * LANES)
    for s in lead:
        n *= s
    return n


def _layer(layer, xs, ps, conv_hist, pool_hist, xp, pp, small, big):
    f32, bf16 = jnp.float32, jnp.bfloat16
    Bs, Ts, D = xs.shape
    Bp, Tp, _ = xp.shape
    Ss, Ls = SAMPLE_TILE
    Sp, Lp = PROMPT_TILE
    assert Ls == Ts and Bs % Ss == 0 and Sp == 1 and Tp % Lp == 0 and min(Ls, Lp) >= POOL_HIST
    sample_steps = Bs // Ss
    tiles_per_seq = Tp // Lp
    grid = (sample_steps + Bp * tiles_per_seq,)

    def sample_block(j):
        return jnp.minimum(j, sample_steps - 1)

    def prompt_block(j):
        q = jnp.maximum(j - sample_steps, 0)
        return q // tiles_per_seq, q % tiles_per_seq

    def sample_rows(width):
        return pl.BlockSpec((Ss, Ls, width), lambda j: (sample_block(j), 0, 0))

    def prompt_rows(width):
        return pl.BlockSpec((Sp, Lp, width), lambda j: prompt_block(j) + (0,))

    def sample_conv(depth_index):
        return pl.BlockSpec((None, Ss, CONV_W - 1, D),
                            lambda j: (depth_index, sample_block(j), 0, 0))

    def sample_pool(depth_index):
        return pl.BlockSpec((None, POOL_HIST, Ss, D),
                            lambda j: (depth_index, 0, sample_block(j), 0))

    prompt_conv = pl.BlockSpec((None, Sp, CONV_W - 1, D), lambda j: (0, prompt_block(j)[0], 0, 0))
    prompt_pool = pl.BlockSpec((None, POOL_HIST, Bp, D), lambda j: (0, 0, 0, 0))

    def resident(arr):
        return pl.BlockSpec(arr.shape, lambda j: (0,) * arr.ndim, pipeline_mode=pl.Buffered(1))

    g_pre, conv_w, conv_b, pool_scale, g_post = small
    small_specs = [resident(g_pre),
                   pl.BlockSpec((None,) + conv_w.shape[1:], lambda j: (layer, 0, 0),
                                pipeline_mode=pl.Buffered(1)),
                   resident(conv_b), resident(pool_scale), resident(g_post)]
    in_specs = ([sample_rows(D), sample_rows(ps.shape[-1]), sample_conv(layer), sample_pool(layer),
                 prompt_rows(D), prompt_rows(pp.shape[-1])]
                + small_specs + [pl.BlockSpec(memory_space=pl.ANY)] * len(big))
    out_specs = [sample_rows(D), sample_conv(0), sample_pool(0),
                 prompt_rows(D), prompt_conv, prompt_pool]
    out_shape = [jax.ShapeDtypeStruct((Bs, Ts, D), f32),
                 jax.ShapeDtypeStruct((1, Bs, CONV_W - 1, D), f32),
                 jax.ShapeDtypeStruct((1, POOL_HIST, Bs, D), f32),
                 jax.ShapeDtypeStruct((Bp, Tp, D), f32),
                 jax.ShapeDtypeStruct((1, Bp, CONV_W - 1, D), f32),
                 jax.ShapeDtypeStruct((1, POOL_HIST, Bp, D), f32)]
    scratch = [(w.shape[1:], bf16) for w in big]
    scratch += [((STAGE_SLOTS, STAGE_ROWS, STAGE_COLS), f32)]
    buffers = [((Ss, CONV_PAD + Ls, D), f32), ((Ss, POOL_PAD + Ls, D), f32),
               ((Sp, CONV_PAD + Lp, D), f32), ((Sp, POOL_PAD + Lp, D), f32)]
    scratch_shapes = ([pltpu.VMEM(s, d) for s, d in scratch]
                      + [pltpu.SemaphoreType.DMA((STAGE_SLOTS,))]
                      + [pltpu.VMEM(s, d) for s, d in buffers])

    blocks = [((Ss, Ls, D), 2), ((Ss, Ls, ps.shape[-1]), 1), ((Ss, CONV_W - 1, D), 2),
              ((POOL_HIST, Ss, D), 2), ((Sp, Lp, D), 2), ((Sp, Lp, pp.shape[-1]), 1),
              ((Sp, CONV_W - 1, D), 1), ((POOL_HIST, Bp, D), 1)]
    vmem_estimate = (sum(_vmem_bytes(s, d) for s, d in scratch + buffers)
                     + sum(2 * n * _vmem_bytes(s, f32) for s, n in blocks)
                     + sum(_vmem_bytes(a.shape[-2:], f32) for a in small)
                     + 2 * LIVE_RESULTS_PER_CHAIN * _vmem_bytes((CHAIN_ROWS, D), f32))
    assert vmem_estimate <= V7X_VMEM_BYTES, vmem_estimate

    return pl.pallas_call(
        functools.partial(_fused_kernel, layer=layer, sample_steps=sample_steps,
                          prompt_tiles_per_seq=tiles_per_seq),
        grid=grid,
        in_specs=in_specs,
        out_specs=out_specs,
        out_shape=out_shape,
        scratch_shapes=scratch_shapes,
        compiler_params=pltpu.CompilerParams(
            dimension_semantics=("arbitrary",),
            vmem_limit_bytes=vmem_estimate),
    )(xs, ps, conv_hist, pool_hist, xp, pp, *small, *big)


def kernel(x_prompt, x_sample, p_prompt, p_sample, cache_conv, state_pool, g_pre, w_in,
           conv_w, conv_b, w_grp, pool_scale, w_a_out, w_b_out, w_o, g_post, w_ple, w_pgate):
    depth = w_in.shape[0]
    hp, hs = x_prompt, x_sample
    big = (w_in, w_grp, w_a_out, w_b_out, w_o, w_ple, w_pgate)
    pool_hist = jnp.transpose(state_pool, STATE_TIME_MAJOR)
    conv_p, conv_s, pool_p, pool_s = [], [], [], []
    for i in range(depth):
        small = (g_pre[i][None, :], conv_w, conv_b[i][None, :], pool_scale[i][None, :],
                 g_post[i][None, :])
        hs, nc_s, np_s, hp, nc_p, np_p = _layer(i, hs, p_sample[i], cache_conv, pool_hist,
                                                hp, p_prompt[i], small, big)
        conv_p.append(nc_p); pool_p.append(jnp.transpose(np_p, STATE_TIME_MAJOR))
        conv_s.append(nc_s); pool_s.append(jnp.transpose(np_s, STATE_TIME_MAJOR))

    def stacked(parts):
        return parts[0] if depth == 1 else jnp.concatenate(parts, axis=0)

    return (hp, hs, stacked(conv_p), stacked(conv_s), stacked(pool_p), stacked(pool_s))
```

```python
import functools
from typing import Any, NamedTuple

import jax
import jax.numpy as jnp
from jax import lax
from jax.experimental import pallas as pl
from jax.experimental.pallas import tpu as pltpu

D_MODEL = 1024
CONV_W = 3
POOL_WINDOWS = (2, 4, 8, 16)
N_POOL_GROUPS = len(POOL_WINDOWS)
POOL_GW = D_MODEL // N_POOL_GROUPS
POOL_HIST = max(POOL_WINDOWS) - 1
PAST_LEN = 4096
EPS = 1e-6

SEG_XA, SEG_BA, SEG_CA, SEG_ZA, SEG_XB, SEG_ZB, SEG_GA, SEG_GB = range(8)

CONV_PAD = 8
POOL_PAD = 16
assert CONV_PAD >= CONV_W - 1 and POOL_PAD >= POOL_HIST

SUBLANES, LANES = 8, 128
V7X_VMEM_BYTES = 64 * 1024 * 1024
LIVE_RESULTS_PER_CHAIN = 3

PROMPT_TILE = (1, 512)
SAMPLE_TILE = (8, 32)
CHAIN_ROWS = 256

STAGE_ROWS, STAGE_COLS, STAGE_SLOTS = 256, 1024, 4

MATS = ("w_in", "w_grp", "w_a", "w_b", "w_o", "w_ple", "w_pgate")

STATE_TIME_MAJOR = (0, 2, 1, 3)


class _Weights(NamedTuple):
    g_pre: Any
    conv_w: Any
    conv_b: Any
    pool_scale: Any
    g_post: Any
    mats: Any

    def dot(self, lhs, name, *, group=None, col0=0, ncols=None):
        ref = self.mats[name] if group is None else self.mats[name].at[group]
        ncols = ref.shape[1] if ncols is None else ncols
        return jnp.dot(lhs, ref[:, col0:col0 + ncols], preferred_element_type=jnp.float32)


def _weight_chunks(layer, hbm, mats):
    chunks = []
    for name in MATS:
        src, dst = hbm[name], mats[name]
        if dst.ndim == 3:
            for g in range(dst.shape[0]):
                chunks.append((src.at[layer, g], dst.at[g], dst.shape[2]))
            continue
        rows, cols = dst.shape
        for c0 in range(0, cols, STAGE_COLS):
            for r0 in range(0, rows, STAGE_ROWS):
                window = (pl.ds(r0, STAGE_ROWS), pl.ds(c0, STAGE_COLS))
                chunks.append((src.at[(layer,) + window], dst.at[window], STAGE_COLS))
    return chunks


def _load_weights(layer, hbm, mats, stage, sem):
    chunks = _weight_chunks(layer, hbm, mats)

    def copy(i):
        src, _, cols = chunks[i]
        slot = i % STAGE_SLOTS
        return pltpu.make_async_copy(src, stage.at[slot, :, pl.ds(0, cols)], sem.at[slot])

    for i in range(min(STAGE_SLOTS, len(chunks))):
        copy(i).start()
    for i, (_, dst, cols) in enumerate(chunks):
        copy(i).wait()
        dst[...] = stage[i % STAGE_SLOTS, :, 0:cols].astype(dst.dtype)
        if i + STAGE_SLOTS < len(chunks):
            copy(i + STAGE_SLOTS).start()


def _rmsnorm(x, g):
    ms = jnp.mean(x * x, axis=-1, keepdims=True)
    return x * lax.rsqrt(ms + EPS) * g


def _sigmoid(x):
    return 0.5 * jnp.tanh(0.5 * x) + 0.5


def _silu(x):
    half = 0.5 * x
    return half * jnp.tanh(half) + half


def _shift_rows(v, k):
    return pltpu.roll(v, k, axis=v.ndim - 2)


def _trailing_sum(v, w):
    assert w & (w - 1) == 0
    k = 1
    while k < w:
        v = v + _shift_rows(v, k)
        k *= 2
    return v


_TAIL = "tail"
_STAY = "stay"


def _chain(s0, r0, Ss, R, pos0, x_ref, p_ref, y_ref, ubuf, xbuf, W, embed_first):
    D = x_ref.shape[-1]
    M = Ss * R
    f32, bf16 = jnp.float32, jnp.bfloat16
    streams = slice(s0, s0 + Ss)

    def embed():
        p = p_ref[streams, r0:r0 + R, :].reshape(M, p_ref.shape[-1])
        return W.dot(p.astype(bf16), "w_ple")

    if embed_first:
        pe = embed()
        yield _STAY
    x = x_ref[streams, r0:r0 + R, :].reshape(M, D)
    h = _rmsnorm(x, W.g_pre[...]).astype(bf16)
    yield

    def proj(seg):
        return W.dot(h, "w_in", col0=seg * D, ncols=D)

    xb = proj(SEG_XB).reshape(Ss, R, D)
    yield
    xbuf[streams, POOL_PAD + r0:POOL_PAD + r0 + R, :] = xb
    ca = proj(SEG_CA)
    yield
    u = (ca * proj(SEG_XA)).reshape(Ss, R, D)
    yield
    ubuf[streams, CONV_PAD + r0:CONV_PAD + r0 + R, :] = u

    pos = lax.broadcasted_iota(jnp.int32, (Ss, R, POOL_GW), 1) + (pos0 + r0)
    d = []
    for g, w in enumerate(POOL_WINDOWS):
        cols = slice(g * POOL_GW, (g + 1) * POOL_GW)
        xg = xb[:, :, cols]
        ext = jnp.concatenate([xbuf[streams, r0:r0 + POOL_PAD, cols], xg], axis=1)
        s = _trailing_sum(ext, w)[:, POOL_PAD:, :]
        cnt = jnp.minimum(pos + 1, w).astype(f32)
        d.append((s / cnt - xg).reshape(M, POOL_GW).astype(bf16))

    ext = jnp.concatenate([ubuf[streams, r0:r0 + CONV_PAD, :], u], axis=1)
    conv = W.conv_b[...] + W.conv_w[CONV_W - 1:CONV_W, :] * u
    for k in range(CONV_W - 1):
        shifted = _shift_rows(ext, CONV_W - 1 - k)[:, CONV_PAD:, :]
        conv = conv + W.conv_w[k:k + 1, :] * shifted
    ba = proj(SEG_BA)
    yield
    a = ((ba * conv.reshape(M, D)) * _silu(proj(SEG_ZA))).astype(bf16)
    yield
    zb = proj(SEG_ZB)
    yield
    mixed = []
    for g in range(N_POOL_GROUPS):
        mixed.append(W.dot(d[g], "w_grp", group=g))
        yield
    mixed = jnp.concatenate(mixed, axis=-1) * W.pool_scale[...]
    b = (mixed * _silu(zb)).astype(bf16)
    ya = W.dot(a, "w_a")
    yield
    m = _sigmoid(proj(SEG_GA)) * ya
    yield
    yb = W.dot(b, "w_b")
    yield _TAIL
    m = (m + _sigmoid(proj(SEG_GB)) * yb).astype(bf16)
    yield
    if not embed_first:
        pe = embed()
        yield

    n_parts = 1 if embed_first else 2
    parts = [(k * M // n_parts, (k + 1) * M // n_parts) for k in range(n_parts)]
    o = []
    for lo, hi in parts:
        o.append(W.dot(m[lo:hi], "w_o"))
        yield
    for (lo, hi), o_part in zip(parts, o):
        x1 = x[lo:hi] + _rmsnorm(o_part, W.g_post[...])
        gate = W.dot(x1.astype(bf16), "w_pgate")
        yield
        y = x1 + pe[lo:hi] * _sigmoid(gate)
        if Ss > 1:
            y_ref[s0 + lo // R:s0 + hi // R, r0:r0 + R, :] = y.reshape((hi - lo) // R, R, D)
        else:
            y_ref[streams, r0 + lo:r0 + hi, :] = y.reshape(1, hi - lo, D)


def _run_interleaved(chains):
    done = object()
    prev = None
    for cur in chains:
        for marker in cur:
            if marker is _STAY:
                continue
            if prev is not None and next(prev, done) is done:
                prev = None
            if marker is _TAIL:
                break
        if prev is not None:
            for _ in prev:
                pass
        prev = cur
    for _ in prev:
        pass


def _chain_shape(S, L):
    if L >= CHAIN_ROWS:
        assert L % CHAIN_ROWS == 0
        return 1, CHAIN_ROWS
    assert CHAIN_ROWS % L == 0
    return min(S, CHAIN_ROWS // L), L


def _tile(x_ref, p_ref, y_ref, ubuf, xbuf, W, *, t, pos0, load_history):
    S, L, _ = x_ref.shape
    Ss, R = _chain_shape(S, L)

    @pl.when(t == 0)
    def _():
        ubuf[:, 0:CONV_PAD, :] = jnp.zeros((S, CONV_PAD, ubuf.shape[-1]), ubuf.dtype)
        xbuf[:, 0:POOL_PAD, :] = jnp.zeros((S, POOL_PAD, xbuf.shape[-1]), xbuf.dtype)
        if load_history is not None:
            load_history()

    starts = [(s0, r0) for s0 in range(0, S, Ss) for r0 in range(0, L, R)]
    _run_interleaved(_chain(s0, r0, Ss, R, pos0, x_ref, p_ref, y_ref, ubuf, xbuf, W,
                            embed_first=i < len(starts) - 1)
                     for i, (s0, r0) in enumerate(starts))

    ubuf[:, CONV_PAD - (CONV_W - 1):CONV_PAD, :] = ubuf[:, CONV_PAD + L - (CONV_W - 1):CONV_PAD + L, :]
    xbuf[:, POOL_PAD - POOL_HIST:POOL_PAD, :] = xbuf[:, POOL_PAD + L - POOL_HIST:POOL_PAD + L, :]


def _fused_kernel(xs_ref, ps_ref, ch_ref, ph_ref, xp_ref, pp_ref,
                  g_pre_ref, conv_w_ref, conv_b_ref, pool_scale_ref, g_post_ref,
                  w_in_hbm, w_grp_hbm, w_a_hbm, w_b_hbm, w_o_hbm, w_ple_hbm, w_pgate_hbm,
                  ys_ref, ncs_ref, nps_ref, yp_ref, ncp_ref, npp_ref,
                  w_in, w_grp, w_a, w_b, w_o, w_ple, w_pgate, stage, sem,
                  ubuf_s, xbuf_s, ubuf_p, xbuf_p, *, layer, sample_steps, prompt_tiles_per_seq):
    j = pl.program_id(0)
    hbm = dict(zip(MATS, (w_in_hbm, w_grp_hbm, w_a_hbm, w_b_hbm, w_o_hbm, w_ple_hbm, w_pgate_hbm)))
    mats = dict(zip(MATS, (w_in, w_grp, w_a, w_b, w_o, w_ple, w_pgate)))
    W = _Weights(g_pre_ref, conv_w_ref, conv_b_ref, pool_scale_ref, g_post_ref, mats)
    Ls, Lp = xs_ref.shape[1], xp_ref.shape[1]

    @pl.when(j == 0)
    def _():
        _load_weights(layer, hbm, mats, stage, sem)

    def sample_history():
        ubuf_s[:, CONV_PAD - (CONV_W - 1):CONV_PAD, :] = ch_ref[...]
        for k in range(POOL_HIST):
            xbuf_s[:, POOL_PAD - POOL_HIST + k, :] = ph_ref[k]

    @pl.when(j < sample_steps)
    def _():
        _tile(xs_ref, ps_ref, ys_ref, ubuf_s, xbuf_s, W, t=0, pos0=PAST_LEN,
              load_history=sample_history)
        ncs_ref[...] = ubuf_s[:, CONV_PAD + Ls - (CONV_W - 1):CONV_PAD + Ls, :]
        for k in range(POOL_HIST):
            nps_ref[k] = xbuf_s[:, POOL_PAD + Ls - POOL_HIST + k, :]

    @pl.when(j == sample_steps)
    def _():
        npp_ref[...] = jnp.zeros(npp_ref.shape, npp_ref.dtype)

    @pl.when(j >= sample_steps)
    def _():
        q = j - sample_steps
        b, t = q // prompt_tiles_per_seq, lax.rem(q, prompt_tiles_per_seq)
        _tile(xp_ref, pp_ref, yp_ref, ubuf_p, xbuf_p, W, t=t, pos0=t * Lp, load_history=None)
        ncp_ref[...] = ubuf_p[:, CONV_PAD + Lp - (CONV_W - 1):CONV_PAD + Lp, :]

        @pl.when(t == prompt_tiles_per_seq - 1)
        def _():
            row_shape = npp_ref.shape[1:]
            mine = lax.broadcasted_iota(jnp.int32, row_shape, 0) == b
            for k in range(POOL_HIST):
                row = xbuf_p[0, POOL_PAD + Lp - POOL_HIST + k:POOL_PAD + Lp - POOL_HIST + k + 1, :]
                npp_ref[k] = jnp.where(mine, jnp.broadcast_to(row, row_shape), npp_ref[k])


def _vmem_bytes(shape, dtype):
    itemsize = jnp.dtype(dtype).itemsize
    sublanes = SUBLANES * 4 // itemsize
    *lead, rows, cols = shape
    n = itemsize * (-(-rows // sublanes) * sublanes) * (-(-cols // LANES) * LANES)
    for s in lead:
        n *= s
    return n


def _layer(layer, xs, ps, conv_hist, pool_hist, xp, pp, small, big):
    f32, bf16 = jnp.float32, jnp.bfloat16
    Bs, Ts, D = xs.shape
    Bp, Tp, _ = xp.shape
    Ss, Ls = SAMPLE_TILE
    Sp, Lp = PROMPT_TILE
    assert Ls == Ts and Bs % Ss == 0 and Sp == 1 and Tp % Lp == 0 and min(Ls, Lp) >= POOL_HIST
    sample_steps = Bs // Ss
    tiles_per_seq = Tp // Lp
    grid = (sample_steps + Bp * tiles_per_seq,)

    def sample_block(j):
        return jnp.minimum(j, sample_steps - 1)

    def prompt_block(j):
        q = jnp.maximum(j - sample_steps, 0)
        return q // tiles_per_seq, q % tiles_per_seq

    def sample_rows(width):
        return pl.BlockSpec((Ss, Ls, width), lambda j: (sample_block(j), 0, 0))

    def prompt_rows(width):
        return pl.BlockSpec((Sp, Lp, width), lambda j: prompt_block(j) + (0,))

    def sample_conv(depth_index):
        return pl.BlockSpec((None, Ss, CONV_W - 1, D),
                            lambda j: (depth_index, sample_block(j), 0, 0))

    def sample_pool(depth_index):
        return pl.BlockSpec((None, POOL_HIST, Ss, D),
                            lambda j: (depth_index, 0, sample_block(j), 0))

    prompt_conv = pl.BlockSpec((None, Sp, CONV_W - 1, D), lambda j: (0, prompt_block(j)[0], 0, 0))
    prompt_pool = pl.BlockSpec((None, POOL_HIST, Bp, D), lambda j: (0, 0, 0, 0))

    def resident(arr):
        return pl.BlockSpec(arr.shape, lambda j: (0,) * arr.ndim, pipeline_mode=pl.Buffered(1))

    g_pre, conv_w, conv_b, pool_scale, g_post = small
    small_specs = [resident(g_pre),
                   pl.BlockSpec((None,) + conv_w.shape[1:], lambda j: (layer, 0, 0),
                                pipeline_mode=pl.Buffered(1)),
                   resident(conv_b), resident(pool_scale), resident(g_post)]
    in_specs = ([sample_rows(D), sample_rows(ps.shape[-1]), sample_conv(layer), sample_pool(layer),
                 prompt_rows(D), prompt_rows(pp.shape[-1])]
                + small_specs + [pl.BlockSpec(memory_space=pl.ANY)] * len(big))
    out_specs = [sample_rows(D), sample_conv(0), sample_pool(0),
                 prompt_rows(D), prompt_conv, prompt_pool]
    out_shape = [jax.ShapeDtypeStruct((Bs, Ts, D), f32),
                 jax.ShapeDtypeStruct((1, Bs, CONV_W - 1, D), f32),
                 jax.ShapeDtypeStruct((1, POOL_HIST, Bs, D), f32),
                 jax.ShapeDtypeStruct((Bp, Tp, D), f32),
                 jax.ShapeDtypeStruct((1, Bp, CONV_W - 1, D), f32),
                 jax.ShapeDtypeStruct((1, POOL_HIST, Bp, D), f32)]
    scratch = [(w.shape[1:], bf16) for w in big]
    scratch += [((STAGE_SLOTS, STAGE_ROWS, STAGE_COLS), f32)]
    buffers = [((Ss, CONV_PAD + Ls, D), f32), ((Ss, POOL_PAD + Ls, D), f32),
               ((Sp, CONV_PAD + Lp, D), f32), ((Sp, POOL_PAD + Lp, D), f32)]
    scratch_shapes = ([pltpu.VMEM(s, d) for s, d in scratch]
                      + [pltpu.SemaphoreType.DMA((STAGE_SLOTS,))]
                      + [pltpu.VMEM(s, d) for s, d in buffers])

    blocks = [((Ss, Ls, D), 2), ((Ss, Ls, ps.shape[-1]), 1), ((Ss, CONV_W - 1, D), 2),
              ((POOL_HIST, Ss, D), 2), ((Sp, Lp, D), 2), ((Sp, Lp, pp.shape[-1]), 1),
              ((Sp, CONV_W - 1, D), 1), ((POOL_HIST, Bp, D), 1)]
    vmem_estimate = (sum(_vmem_bytes(s, d) for s, d in scratch + buffers)
                     + sum(2 * n * _vmem_bytes(s, f32) for s, n in blocks)
                     + sum(_vmem_bytes(a.shape[-2:], f32) for a in small)
                     + min(2, Lp // CHAIN_ROWS) * LIVE_RESULTS_PER_CHAIN
                     * _vmem_bytes((CHAIN_ROWS, D), f32))
    assert vmem_estimate <= V7X_VMEM_BYTES, vmem_estimate

    return pl.pallas_call(
        functools.partial(_fused_kernel, layer=layer, sample_steps=sample_steps,
                          prompt_tiles_per_seq=tiles_per_seq),
        grid=grid,
        in_specs=in_specs,
        out_specs=out_specs,
        out_shape=out_shape,
        scratch_shapes=scratch_shapes,
        compiler_params=pltpu.CompilerParams(
            dimension_semantics=("arbitrary",),
            vmem_limit_bytes=vmem_estimate),
    )(xs, ps, conv_hist, pool_hist, xp, pp, *small, *big)


def kernel(x_prompt, x_sample, p_prompt, p_sample, cache_conv, state_pool, g_pre, w_in,
           conv_w, conv_b, w_grp, pool_scale, w_a_out, w_b_out, w_o, g_post, w_ple, w_pgate):
    depth = w_in.shape[0]
    hp, hs = x_prompt, x_sample
    big = (w_in, w_grp, w_a_out, w_b_out, w_o, w_ple, w_pgate)
    pool_hist = jnp.transpose(state_pool, STATE_TIME_MAJOR)
    conv_p, conv_s, pool_p, pool_s = [], [], [], []
    for i in range(depth):
        small = (g_pre[i][None, :], conv_w, conv_b[i][None, :], pool_scale[i][None, :],
                 g_post[i][None, :])
        hs, nc_s, np_s, hp, nc_p, np_p = _layer(i, hs, p_sample[i], cache_conv, pool_hist,
                                                hp, p_prompt[i], small, big)
        conv_p.append(nc_p); pool_p.append(jnp.transpose(np_p, STATE_TIME_MAJOR))
        conv_s.append(nc_s); pool_s.append(jnp.transpose(np_s, STATE_TIME_MAJOR))

    def stacked(parts):
        return parts[0] if depth == 1 else jnp.concatenate(parts, axis=0)

    return (hp, hs, stacked(conv_p), stacked(conv_s), stacked(pool_p), stacked(pool_s))
```

```python
import functools
from typing import Any, NamedTuple

import jax
import jax.numpy as jnp
from jax import lax
from jax.experimental import pallas as pl
from jax.experimental.pallas import tpu as pltpu

D_MODEL = 1024
CONV_W = 3
POOL_WINDOWS = (2, 4, 8, 16)
N_POOL_GROUPS = len(POOL_WINDOWS)
POOL_GW = D_MODEL // N_POOL_GROUPS
POOL_HIST = max(POOL_WINDOWS) - 1
PAST_LEN = 4096
EPS = 1e-6

SEG_XA, SEG_BA, SEG_CA, SEG_ZA, SEG_XB, SEG_ZB, SEG_GA, SEG_GB = range(8)

CONV_PAD = 8
POOL_PAD = 16
assert CONV_PAD >= CONV_W - 1 and POOL_PAD >= POOL_HIST

SUBLANES, LANES = 8, 128
V7X_VMEM_BYTES = 64 * 1024 * 1024
LIVE_RESULTS_PER_CHAIN = 3

PROMPT_TILE = (1, 512)
SAMPLE_TILE = (8, 32)
CHAIN_ROWS = 256

STAGE_ROWS, STAGE_COLS, STAGE_SLOTS = 256, 1024, 6

MATS = ("w_in", "w_grp", "w_a", "w_b", "w_o", "w_ple", "w_pgate")

STATE_TIME_MAJOR = (0, 2, 1, 3)


class _Weights(NamedTuple):
    g_pre: Any
    conv_w: Any
    conv_b: Any
    pool_scale: Any
    g_post: Any
    mats: Any
    stream: Any = None

    def dot(self, lhs, name, *, group=None, col0=0, ncols=None):
        ref = self.mats[name] if group is None else self.mats[name].at[group]
        ncols = ref.shape[1] if ncols is None else ncols
        cols = slice(col0, col0 + ncols)
        if self.stream is None:
            return jnp.dot(lhs, ref[:, cols], preferred_element_type=jnp.float32)
        acc = None
        for r0 in range(0, ref.shape[0], STAGE_ROWS):
            self.stream.need(name, group, r0, col0)
            part = jnp.dot(lhs[:, r0:r0 + STAGE_ROWS], ref[r0:r0 + STAGE_ROWS, cols],
                           preferred_element_type=jnp.float32)
            acc = part if acc is None else acc + part
        return acc


class _Unit(NamedTuple):
    name: str
    group: Any
    row0: int
    col0: int
    rows: int
    cols: int


def _stream_units(mats):
    D = D_MODEL

    def chunks(name, col0=0, group=None):
        rows, cols = mats[name].shape[-2:]
        cols = min(cols, STAGE_COLS)
        return [_Unit(name, group, r0, col0, min(rows, STAGE_ROWS), cols)
                for r0 in range(0, rows, STAGE_ROWS)]

    def seg(s):
        return chunks("w_in", s * D)

    units = seg(SEG_XB) + seg(SEG_CA) + seg(SEG_XA) + seg(SEG_BA) + seg(SEG_ZA) + seg(SEG_ZB)
    for g in range(N_POOL_GROUPS):
        units += chunks("w_grp", group=g)
    units += chunks("w_a") + seg(SEG_GA) + chunks("w_b") + seg(SEG_GB)
    units += chunks("w_ple") + chunks("w_o") + chunks("w_pgate")
    return units


class _WeightStream:
    AHEAD = 2

    def __init__(self, layer, hbm, mats, stage, sem):
        self.layer, self.hbm, self.mats, self.stage, self.sem = layer, hbm, mats, stage, sem
        self.units = _stream_units(mats)
        self.index = {u[:4]: i for i, u in enumerate(self.units)}
        self.arrived = 0

    def _copy(self, i):
        u = self.units[i]
        lead = (self.layer,) if u.group is None else (self.layer, u.group)
        src = self.hbm[u.name].at[lead + (pl.ds(u.row0, u.rows), pl.ds(u.col0, u.cols))]
        slot = i % STAGE_SLOTS
        return pltpu.make_async_copy(src, self.stage.at[slot, pl.ds(0, u.rows), pl.ds(0, u.cols)],
                                     self.sem.at[slot])

    def start(self):
        for i in range(min(STAGE_SLOTS, len(self.units))):
            self._copy(i).start()

    def _arrive(self):
        i = self.arrived
        u = self.units[i]
        self._copy(i).wait()
        dst = self.mats[u.name] if u.group is None else self.mats[u.name].at[u.group]
        dst[u.row0:u.row0 + u.rows, u.col0:u.col0 + u.cols] = (
            self.stage[i % STAGE_SLOTS, 0:u.rows, 0:u.cols].astype(dst.dtype))
        if i + STAGE_SLOTS < len(self.units):
            self._copy(i + STAGE_SLOTS).start()
        self.arrived += 1

    def need(self, name, group, row0, col0):
        i = self.index[(name, group, row0, col0)]
        assert i <= self.arrived, "weights are read in another order than they are streamed"
        if self.arrived <= min(i + 1, len(self.units) - 1):
            while self.arrived < min(i + 1 + self.AHEAD, len(self.units)):
                self._arrive()

    def finish(self):
        while self.arrived < len(self.units):
            self._arrive()


def _rmsnorm(x, g):
    ms = jnp.mean(x * x, axis=-1, keepdims=True)
    return x * lax.rsqrt(ms + EPS) * g


def _sigmoid(x):
    return 0.5 * jnp.tanh(0.5 * x) + 0.5


def _silu(x):
    half = 0.5 * x
    return half * jnp.tanh(half) + half


def _shift_rows(v, k):
    return pltpu.roll(v, k, axis=v.ndim - 2)


def _trailing_sum(v, w):
    assert w & (w - 1) == 0
    k = 1
    while k < w:
        v = v + _shift_rows(v, k)
        k *= 2
    return v


_TAIL = "tail"
_STAY = "stay"


def _chain(s0, r0, Ss, R, pos0, x_ref, p_ref, y_ref, ubuf, xbuf, W, embed_first):
    D = x_ref.shape[-1]
    M = Ss * R
    f32, bf16 = jnp.float32, jnp.bfloat16
    streams = slice(s0, s0 + Ss)

    def embed():
        p = p_ref[streams, r0:r0 + R, :].reshape(M, p_ref.shape[-1])
        return W.dot(p.astype(bf16), "w_ple")

    if embed_first:
        pe = embed()
        yield _STAY
    x = x_ref[streams, r0:r0 + R, :].reshape(M, D)
    h = _rmsnorm(x, W.g_pre[...]).astype(bf16)
    yield

    def proj(seg):
        return W.dot(h, "w_in", col0=seg * D, ncols=D)

    xb = proj(SEG_XB).reshape(Ss, R, D)
    yield
    xbuf[streams, POOL_PAD + r0:POOL_PAD + r0 + R, :] = xb
    ca = proj(SEG_CA)
    yield
    u = (ca * proj(SEG_XA)).reshape(Ss, R, D)
    yield
    ubuf[streams, CONV_PAD + r0:CONV_PAD + r0 + R, :] = u

    pos = lax.broadcasted_iota(jnp.int32, (Ss, R, POOL_GW), 1) + (pos0 + r0)
    d = []
    for g, w in enumerate(POOL_WINDOWS):
        cols = slice(g * POOL_GW, (g + 1) * POOL_GW)
        xg = xb[:, :, cols]
        ext = jnp.concatenate([xbuf[streams, r0:r0 + POOL_PAD, cols], xg], axis=1)
        s = _trailing_sum(ext, w)[:, POOL_PAD:, :]
        cnt = jnp.minimum(pos + 1, w).astype(f32)
        d.append((s / cnt - xg).reshape(M, POOL_GW).astype(bf16))

    ext = jnp.concatenate([ubuf[streams, r0:r0 + CONV_PAD, :], u], axis=1)
    conv = W.conv_b[...] + W.conv_w[CONV_W - 1:CONV_W, :] * u
    for k in range(CONV_W - 1):
        shifted = _shift_rows(ext, CONV_W - 1 - k)[:, CONV_PAD:, :]
        conv = conv + W.conv_w[k:k + 1, :] * shifted
    ba = proj(SEG_BA)
    yield
    a = ((ba * conv.reshape(M, D)) * _silu(proj(SEG_ZA))).astype(bf16)
    yield
    zb = proj(SEG_ZB)
    yield
    mixed = []
    for g in range(N_POOL_GROUPS):
        mixed.append(W.dot(d[g], "w_grp", group=g))
        yield
    mixed = jnp.concatenate(mixed, axis=-1) * W.pool_scale[...]
    b = (mixed * _silu(zb)).astype(bf16)
    ya = W.dot(a, "w_a")
    yield
    m = _sigmoid(proj(SEG_GA)) * ya
    yield
    yb = W.dot(b, "w_b")
    yield _TAIL
    m = (m + _sigmoid(proj(SEG_GB)) * yb).astype(bf16)
    yield
    if not embed_first:
        pe = embed()
        yield

    n_parts = 1 if embed_first else 2
    parts = [(k * M // n_parts, (k + 1) * M // n_parts) for k in range(n_parts)]
    o = []
    for lo, hi in parts:
        o.append(W.dot(m[lo:hi], "w_o"))
        yield
    for (lo, hi), o_part in zip(parts, o):
        x1 = x[lo:hi] + _rmsnorm(o_part, W.g_post[...])
        gate = W.dot(x1.astype(bf16), "w_pgate")
        yield
        y = x1 + pe[lo:hi] * _sigmoid(gate)
        if Ss > 1:
            y_ref[s0 + lo // R:s0 + hi // R, r0:r0 + R, :] = y.reshape((hi - lo) // R, R, D)
        else:
            y_ref[streams, r0 + lo:r0 + hi, :] = y.reshape(1, hi - lo, D)


def _run_interleaved(chains):
    done = object()
    prev = None
    for cur in chains:
        for marker in cur:
            if marker is _STAY:
                continue
            if prev is not None and next(prev, done) is done:
                prev = None
            if marker is _TAIL:
                break
        if prev is not None:
            for _ in prev:
                pass
        prev = cur
    for _ in prev:
        pass


def _chain_shape(S, L):
    if L >= CHAIN_ROWS:
        assert L % CHAIN_ROWS == 0
        return 1, CHAIN_ROWS
    assert CHAIN_ROWS % L == 0
    return min(S, CHAIN_ROWS // L), L


def _tile(x_ref, p_ref, y_ref, ubuf, xbuf, W, *, t, pos0, load_history):
    S, L, _ = x_ref.shape
    Ss, R = _chain_shape(S, L)

    @pl.when(t == 0)
    def _():
        ubuf[:, 0:CONV_PAD, :] = jnp.zeros((S, CONV_PAD, ubuf.shape[-1]), ubuf.dtype)
        xbuf[:, 0:POOL_PAD, :] = jnp.zeros((S, POOL_PAD, xbuf.shape[-1]), xbuf.dtype)
        if load_history is not None:
            load_history()

    starts = [(s0, r0) for s0 in range(0, S, Ss) for r0 in range(0, L, R)]
    assert W.stream is None or len(starts) == 1
    _run_interleaved(_chain(s0, r0, Ss, R, pos0, x_ref, p_ref, y_ref, ubuf, xbuf, W,
                            embed_first=i < len(starts) - 1)
                     for i, (s0, r0) in enumerate(starts))

    ubuf[:, CONV_PAD - (CONV_W - 1):CONV_PAD, :] = ubuf[:, CONV_PAD + L - (CONV_W - 1):CONV_PAD + L, :]
    xbuf[:, POOL_PAD - POOL_HIST:POOL_PAD, :] = xbuf[:, POOL_PAD + L - POOL_HIST:POOL_PAD + L, :]


def _fused_kernel(xs_ref, ps_ref, ch_ref, ph_ref, xp_ref, pp_ref,
                  g_pre_ref, conv_w_ref, conv_b_ref, pool_scale_ref, g_post_ref,
                  w_in_hbm, w_grp_hbm, w_a_hbm, w_b_hbm, w_o_hbm, w_ple_hbm, w_pgate_hbm,
                  ys_ref, ncs_ref, nps_ref, yp_ref, ncp_ref, npp_ref,
                  w_in, w_grp, w_a, w_b, w_o, w_ple, w_pgate, stage, sem,
                  ubuf_s, xbuf_s, ubuf_p, xbuf_p, *, layer, sample_steps, prompt_tiles_per_seq):
    j = pl.program_id(0)
    hbm = dict(zip(MATS, (w_in_hbm, w_grp_hbm, w_a_hbm, w_b_hbm, w_o_hbm, w_ple_hbm, w_pgate_hbm)))
    mats = dict(zip(MATS, (w_in, w_grp, w_a, w_b, w_o, w_ple, w_pgate)))
    W = _Weights(g_pre_ref, conv_w_ref, conv_b_ref, pool_scale_ref, g_post_ref, mats)
    Ls, Lp = xs_ref.shape[1], xp_ref.shape[1]

    def sample_history():
        ubuf_s[:, CONV_PAD - (CONV_W - 1):CONV_PAD, :] = ch_ref[...]
        for k in range(POOL_HIST):
            xbuf_s[:, POOL_PAD - POOL_HIST + k, :] = ph_ref[k]

    def sample_tile(weights):
        _tile(xs_ref, ps_ref, ys_ref, ubuf_s, xbuf_s, weights, t=0, pos0=PAST_LEN,
              load_history=sample_history)
        ncs_ref[...] = ubuf_s[:, CONV_PAD + Ls - (CONV_W - 1):CONV_PAD + Ls, :]
        for k in range(POOL_HIST):
            nps_ref[k] = xbuf_s[:, POOL_PAD + Ls - POOL_HIST + k, :]

    @pl.when(j == 0)
    def _():
        stream = _WeightStream(layer, hbm, mats, stage, sem)
        stream.start()
        sample_tile(W._replace(stream=stream))
        stream.finish()

    @pl.when((j > 0) & (j < sample_steps))
    def _():
        sample_tile(W)

    @pl.when(j == sample_steps)
    def _():
        npp_ref[...] = jnp.zeros(npp_ref.shape, npp_ref.dtype)

    @pl.when(j >= sample_steps)
    def _():
        q = j - sample_steps
        b, t = q // prompt_tiles_per_seq, lax.rem(q, prompt_tiles_per_seq)
        _tile(xp_ref, pp_ref, yp_ref, ubuf_p, xbuf_p, W, t=t, pos0=t * Lp, load_history=None)
        ncp_ref[...] = ubuf_p[:, CONV_PAD + Lp - (CONV_W - 1):CONV_PAD + Lp, :]

        @pl.when(t == prompt_tiles_per_seq - 1)
        def _():
            row_shape = npp_ref.shape[1:]
            mine = lax.broadcasted_iota(jnp.int32, row_shape, 0) == b
            for k in range(POOL_HIST):
                row = xbuf_p[0, POOL_PAD + Lp - POOL_HIST + k:POOL_PAD + Lp - POOL_HIST + k + 1, :]
                npp_ref[k] = jnp.where(mine, jnp.broadcast_to(row, row_shape), npp_ref[k])


def _vmem_bytes(shape, dtype):
    itemsize = jnp.dtype(dtype).itemsize
    sublanes = SUBLANES * 4 // itemsize
    *lead, rows, cols = shape
    n = itemsize * (-(-rows // sublanes) * sublanes) * (-(-cols // LANES) * LANES)
    for s in lead:
        n *= s
    return n


def _layer(layer, xs, ps, conv_hist, pool_hist, xp, pp, small, big):
    f32, bf16 = jnp.float32, jnp.bfloat16
    Bs, Ts, D = xs.shape
    Bp, Tp, _ = xp.shape
    Ss, Ls = SAMPLE_TILE
    Sp, Lp = PROMPT_TILE
    assert Ls == Ts and Bs % Ss == 0 and Sp == 1 and Tp % Lp == 0 and min(Ls, Lp) >= POOL_HIST
    sample_steps = Bs // Ss
    tiles_per_seq = Tp // Lp
    grid = (sample_steps + Bp * tiles_per_seq,)

    def sample_block(j):
        return jnp.minimum(j, sample_steps - 1)

    def prompt_block(j):
        q = jnp.maximum(j - sample_steps, 0)
        return q // tiles_per_seq, q % tiles_per_seq

    def sample_rows(width):
        return pl.BlockSpec((Ss, Ls, width), lambda j: (sample_block(j), 0, 0))

    def prompt_rows(width):
        return pl.BlockSpec((Sp, Lp, width), lambda j: prompt_block(j) + (0,))

    def sample_conv(depth_index):
        return pl.BlockSpec((None, Ss, CONV_W - 1, D),
                            lambda j: (depth_index, sample_block(j), 0, 0))

    def sample_pool(depth_index):
        return pl.BlockSpec((None, POOL_HIST, Ss, D),
                            lambda j: (depth_index, 0, sample_block(j), 0))

    prompt_conv = pl.BlockSpec((None, Sp, CONV_W - 1, D), lambda j: (0, prompt_block(j)[0], 0, 0))
    prompt_pool = pl.BlockSpec((None, POOL_HIST, Bp, D), lambda j: (0, 0, 0, 0))

    def resident(arr):
        return pl.BlockSpec(arr.shape, lambda j: (0,) * arr.ndim, pipeline_mode=pl.Buffered(1))

    g_pre, conv_w, conv_b, pool_scale, g_post = small
    small_specs = [resident(g_pre),
                   pl.BlockSpec((None,) + conv_w.shape[1:], lambda j: (layer, 0, 0),
                                pipeline_mode=pl.Buffered(1)),
                   resident(conv_b), resident(pool_scale), resident(g_post)]
    in_specs = ([sample_rows(D), sample_rows(ps.shape[-1]), sample_conv(layer), sample_pool(layer),
                 prompt_rows(D), prompt_rows(pp.shape[-1])]
                + small_specs + [pl.BlockSpec(memory_space=pl.ANY)] * len(big))
    out_specs = [sample_rows(D), sample_conv(0), sample_pool(0),
                 prompt_rows(D), prompt_conv, prompt_pool]
    out_shape = [jax.ShapeDtypeStruct((Bs, Ts, D), f32),
                 jax.ShapeDtypeStruct((1, Bs, CONV_W - 1, D), f32),
                 jax.ShapeDtypeStruct((1, POOL_HIST, Bs, D), f32),
                 jax.ShapeDtypeStruct((Bp, Tp, D), f32),
                 jax.ShapeDtypeStruct((1, Bp, CONV_W - 1, D), f32),
                 jax.ShapeDtypeStruct((1, POOL_HIST, Bp, D), f32)]
    scratch = [(w.shape[1:], bf16) for w in big]
    scratch += [((STAGE_SLOTS, STAGE_ROWS, STAGE_COLS), f32)]
    buffers = [((Ss, CONV_PAD + Ls, D), f32), ((Ss, POOL_PAD + Ls, D), f32),
               ((Sp, CONV_PAD + Lp, D), f32), ((Sp, POOL_PAD + Lp, D), f32)]
    scratch_shapes = ([pltpu.VMEM(s, d) for s, d in scratch]
                      + [pltpu.SemaphoreType.DMA((STAGE_SLOTS,))]
                      + [pltpu.VMEM(s, d) for s, d in buffers])

    blocks = [((Ss, Ls, D), 2), ((Ss, Ls, ps.shape[-1]), 1), ((Ss, CONV_W - 1, D), 2),
              ((POOL_HIST, Ss, D), 2), ((Sp, Lp, D), 2), ((Sp, Lp, pp.shape[-1]), 1),
              ((Sp, CONV_W - 1, D), 1), ((POOL_HIST, Bp, D), 1)]
    vmem_estimate = (sum(_vmem_bytes(s, d) for s, d in scratch + buffers)
                     + sum(2 * n * _vmem_bytes(s, f32) for s, n in blocks)
                     + sum(_vmem_bytes(a.shape[-2:], f32) for a in small)
                     + min(2, Lp // CHAIN_ROWS) * LIVE_RESULTS_PER_CHAIN
                     * _vmem_bytes((CHAIN_ROWS, D), f32))
    assert vmem_estimate <= V7X_VMEM_BYTES, vmem_estimate

    return pl.pallas_call(
        functools.partial(_fused_kernel, layer=layer, sample_steps=sample_steps,
                          prompt_tiles_per_seq=tiles_per_seq),
        grid=grid,
        in_specs=in_specs,
        out_specs=out_specs,
        out_shape=out_shape,
        scratch_shapes=scratch_shapes,
        compiler_params=pltpu.CompilerParams(
            dimension_semantics=("arbitrary",),
            vmem_limit_bytes=vmem_estimate),
    )(xs, ps, conv_hist, pool_hist, xp, pp, *small, *big)


def kernel(x_prompt, x_sample, p_prompt, p_sample, cache_conv, state_pool, g_pre, w_in,
           conv_w, conv_b, w_grp, pool_scale, w_a_out, w_b_out, w_o, g_post, w_ple, w_pgate):
    depth = w_in.shape[0]
    hp, hs = x_prompt, x_sample
    big = (w_in, w_grp, w_a_out, w_b_out, w_o, w_ple, w_pgate)
    pool_hist = jnp.transpose(state_pool, STATE_TIME_MAJOR)
    conv_p, conv_s, pool_p, pool_s = [], [], [], []
    for i in range(depth):
        small = (g_pre[i][None, :], conv_w, conv_b[i][None, :], pool_scale[i][None, :],
                 g_post[i][None, :])
        hs, nc_s, np_s, hp, nc_p, np_p = _layer(i, hs, p_sample[i], cache_conv, pool_hist,
                                                hp, p_prompt[i], small, big)
        conv_p.append(nc_p); pool_p.append(jnp.transpose(np_p, STATE_TIME_MAJOR))
        conv_s.append(nc_s); pool_s.append(jnp.transpose(np_s, STATE_TIME_MAJOR))

    def stacked(parts):
        return parts[0] if depth == 1 else jnp.concatenate(parts, axis=0)

    return (hp, hs, stacked(conv_p), stacked(conv_s), stacked(pool_p), stacked(pool_s))
```

```python
import functools
from typing import Any, NamedTuple

import jax
import jax.numpy as jnp
from jax import lax
from jax.experimental import pallas as pl
from jax.experimental.pallas import tpu as pltpu

D_MODEL = 1024
CONV_W = 3
POOL_WINDOWS = (2, 4, 8, 16)
N_POOL_GROUPS = len(POOL_WINDOWS)
POOL_GW = D_MODEL // N_POOL_GROUPS
POOL_HIST = max(POOL_WINDOWS) - 1
PAST_LEN = 4096
EPS = 1e-6

SEG_XA, SEG_BA, SEG_CA, SEG_ZA, SEG_XB, SEG_ZB, SEG_GA, SEG_GB = range(8)

CONV_PAD = 8
POOL_PAD = 16
assert CONV_PAD >= CONV_W - 1 and POOL_PAD >= POOL_HIST

SUBLANES, LANES = 8, 128
V7X_VMEM_BYTES = 64 * 1024 * 1024
LIVE_RESULTS_PER_CHAIN = 3

PROMPT_TILE = (1, 512)
SAMPLE_TILE = (8, 32)
CHAIN_ROWS = 256

STAGE_ROWS, STAGE_COLS, STAGE_SLOTS = 256, 1024, 6

MATS = ("w_in", "w_grp", "w_a", "w_b", "w_o", "w_ple", "w_pgate")

STATE_TIME_MAJOR = (0, 2, 1, 3)


class _Weights(NamedTuple):
    g_pre: Any
    conv_w: Any
    conv_b: Any
    pool_scale: Any
    g_post: Any
    mats: Any
    stream: Any = None

    def dot(self, lhs, name, *, group=None, col0=0, ncols=None):
        ref = self.mats[name] if group is None else self.mats[name].at[group]
        ncols = ref.shape[1] if ncols is None else ncols
        cols = slice(col0, col0 + ncols)
        if self.stream is None:
            return jnp.dot(lhs, ref[:, cols], preferred_element_type=jnp.float32)
        acc = None
        for r0 in range(0, ref.shape[0], STAGE_ROWS):
            self.stream.need(name, group, r0, col0)
            part = jnp.dot(lhs[:, r0:r0 + STAGE_ROWS], ref[r0:r0 + STAGE_ROWS, cols],
                           preferred_element_type=jnp.float32)
            acc = part if acc is None else acc + part
        return acc


class _Unit(NamedTuple):
    name: str
    group: Any
    row0: int
    col0: int
    rows: int
    cols: int


def _stream_units(mats):
    D = D_MODEL

    def chunks(name, col0=0, group=None):
        rows, cols = mats[name].shape[-2:]
        cols = min(cols, STAGE_COLS)
        return [_Unit(name, group, r0, col0, min(rows, STAGE_ROWS), cols)
                for r0 in range(0, rows, STAGE_ROWS)]

    def seg(s):
        return chunks("w_in", s * D)

    units = seg(SEG_XB) + seg(SEG_CA) + seg(SEG_XA) + seg(SEG_BA) + seg(SEG_ZA) + seg(SEG_ZB)
    for g in range(N_POOL_GROUPS):
        units += chunks("w_grp", group=g)
    units += chunks("w_a") + seg(SEG_GA) + chunks("w_b") + seg(SEG_GB)
    units += chunks("w_ple") + chunks("w_o") + chunks("w_pgate")
    return units


class _WeightStream:
    AHEAD = 4

    def __init__(self, layer, hbm, mats, stage, sem):
        self.layer, self.hbm, self.mats, self.stage, self.sem = layer, hbm, mats, stage, sem
        self.units = _stream_units(mats)
        self.index = {u[:4]: i for i, u in enumerate(self.units)}
        self.arrived = 0

    def _copy(self, i):
        u = self.units[i]
        lead = (self.layer,) if u.group is None else (self.layer, u.group)
        src = self.hbm[u.name].at[lead + (pl.ds(u.row0, u.rows), pl.ds(u.col0, u.cols))]
        slot = i % STAGE_SLOTS
        return pltpu.make_async_copy(src, self.stage.at[slot, pl.ds(0, u.rows), pl.ds(0, u.cols)],
                                     self.sem.at[slot])

    def start(self):
        for i in range(min(STAGE_SLOTS, len(self.units))):
            self._copy(i).start()

    def _arrive(self):
        i = self.arrived
        u = self.units[i]
        self._copy(i).wait()
        dst = self.mats[u.name] if u.group is None else self.mats[u.name].at[u.group]
        dst[u.row0:u.row0 + u.rows, u.col0:u.col0 + u.cols] = (
            self.stage[i % STAGE_SLOTS, 0:u.rows, 0:u.cols].astype(dst.dtype))
        if i + STAGE_SLOTS < len(self.units):
            self._copy(i + STAGE_SLOTS).start()
        self.arrived += 1

    def need(self, name, group, row0, col0):
        i = self.index[(name, group, row0, col0)]
        assert i <= self.arrived, "weights are read in another order than they are streamed"
        if self.arrived <= min(i + 1, len(self.units) - 1):
            while self.arrived < min(i + 1 + self.AHEAD, len(self.units)):
                self._arrive()

    def finish(self):
        while self.arrived < len(self.units):
            self._arrive()


def _rmsnorm(x, g):
    ms = jnp.mean(x * x, axis=-1, keepdims=True)
    return x * lax.rsqrt(ms + EPS) * g


def _sigmoid(x):
    return 0.5 * jnp.tanh(0.5 * x) + 0.5


def _silu(x):
    half = 0.5 * x
    return half * jnp.tanh(half) + half


def _shift_rows(v, k):
    return pltpu.roll(v, k, axis=v.ndim - 2)


def _trailing_sum(v, w):
    assert w & (w - 1) == 0
    k = 1
    while k < w:
        v = v + _shift_rows(v, k)
        k *= 2
    return v


_TAIL = "tail"
_STAY = "stay"


def _chain(s0, r0, Ss, R, pos0, x_ref, p_ref, y_ref, ubuf, xbuf, W, embed_first):
    D = x_ref.shape[-1]
    M = Ss * R
    f32, bf16 = jnp.float32, jnp.bfloat16
    streams = slice(s0, s0 + Ss)

    def embed():
        p = p_ref[streams, r0:r0 + R, :].reshape(M, p_ref.shape[-1])
        return W.dot(p.astype(bf16), "w_ple")

    if embed_first:
        pe = embed()
        yield _STAY
    x = x_ref[streams, r0:r0 + R, :].reshape(M, D)
    h = _rmsnorm(x, W.g_pre[...]).astype(bf16)
    yield

    def proj(seg):
        return W.dot(h, "w_in", col0=seg * D, ncols=D)

    xb = proj(SEG_XB).reshape(Ss, R, D)
    yield
    xbuf[streams, POOL_PAD + r0:POOL_PAD + r0 + R, :] = xb
    ca = proj(SEG_CA)
    yield
    u = (ca * proj(SEG_XA)).reshape(Ss, R, D)
    yield
    ubuf[streams, CONV_PAD + r0:CONV_PAD + r0 + R, :] = u

    pos = lax.broadcasted_iota(jnp.int32, (Ss, R, POOL_GW), 1) + (pos0 + r0)
    d = []
    for g, w in enumerate(POOL_WINDOWS):
        cols = slice(g * POOL_GW, (g + 1) * POOL_GW)
        xg = xb[:, :, cols]
        ext = jnp.concatenate([xbuf[streams, r0:r0 + POOL_PAD, cols], xg], axis=1)
        s = _trailing_sum(ext, w)[:, POOL_PAD:, :]
        cnt = jnp.minimum(pos + 1, w).astype(f32)
        d.append((s / cnt - xg).reshape(M, POOL_GW).astype(bf16))

    ext = jnp.concatenate([ubuf[streams, r0:r0 + CONV_PAD, :], u], axis=1)
    conv = W.conv_b[...] + W.conv_w[CONV_W - 1:CONV_W, :] * u
    for k in range(CONV_W - 1):
        shifted = _shift_rows(ext, CONV_W - 1 - k)[:, CONV_PAD:, :]
        conv = conv + W.conv_w[k:k + 1, :] * shifted
    ba = proj(SEG_BA)
    yield
    a = ((ba * conv.reshape(M, D)) * _silu(proj(SEG_ZA))).astype(bf16)
    yield
    zb = proj(SEG_ZB)
    yield
    mixed = []
    for g in range(N_POOL_GROUPS):
        mixed.append(W.dot(d[g], "w_grp", group=g))
        yield
    mixed = jnp.concatenate(mixed, axis=-1) * W.pool_scale[...]
    b = (mixed * _silu(zb)).astype(bf16)
    ya = W.dot(a, "w_a")
    yield
    m = _sigmoid(proj(SEG_GA)) * ya
    yield
    yb = W.dot(b, "w_b")
    yield _TAIL
    m = (m + _sigmoid(proj(SEG_GB)) * yb).astype(bf16)
    yield
    if not embed_first:
        pe = embed()
        yield

    n_parts = 1 if embed_first else 2
    parts = [(k * M // n_parts, (k + 1) * M // n_parts) for k in range(n_parts)]
    o = []
    for lo, hi in parts:
        o.append(W.dot(m[lo:hi], "w_o"))
        yield
    for (lo, hi), o_part in zip(parts, o):
        x1 = x[lo:hi] + _rmsnorm(o_part, W.g_post[...])
        gate = W.dot(x1.astype(bf16), "w_pgate")
        yield
        y = x1 + pe[lo:hi] * _sigmoid(gate)
        if Ss > 1:
            y_ref[s0 + lo // R:s0 + hi // R, r0:r0 + R, :] = y.reshape((hi - lo) // R, R, D)
        else:
            y_ref[streams, r0 + lo:r0 + hi, :] = y.reshape(1, hi - lo, D)


def _run_interleaved(chains):
    done = object()
    prev = None
    for cur in chains:
        for marker in cur:
            if marker is _STAY:
                continue
            if prev is not None and next(prev, done) is done:
                prev = None
            if marker is _TAIL:
                break
        if prev is not None:
            for _ in prev:
                pass
        prev = cur
    for _ in prev:
        pass


def _chain_shape(S, L):
    if L >= CHAIN_ROWS:
        assert L % CHAIN_ROWS == 0
        return 1, CHAIN_ROWS
    assert CHAIN_ROWS % L == 0
    return min(S, CHAIN_ROWS // L), L


def _tile(x_ref, p_ref, y_ref, ubuf, xbuf, W, *, t, pos0, load_history):
    S, L, _ = x_ref.shape
    Ss, R = _chain_shape(S, L)

    @pl.when(t == 0)
    def _():
        ubuf[:, 0:CONV_PAD, :] = jnp.zeros((S, CONV_PAD, ubuf.shape[-1]), ubuf.dtype)
        xbuf[:, 0:POOL_PAD, :] = jnp.zeros((S, POOL_PAD, xbuf.shape[-1]), xbuf.dtype)
        if load_history is not None:
            load_history()

    starts = [(s0, r0) for s0 in range(0, S, Ss) for r0 in range(0, L, R)]
    assert W.stream is None or len(starts) == 1
    _run_interleaved(_chain(s0, r0, Ss, R, pos0, x_ref, p_ref, y_ref, ubuf, xbuf, W,
                            embed_first=i < len(starts) - 1)
                     for i, (s0, r0) in enumerate(starts))

    ubuf[:, CONV_PAD - (CONV_W - 1):CONV_PAD, :] = ubuf[:, CONV_PAD + L - (CONV_W - 1):CONV_PAD + L, :]
    xbuf[:, POOL_PAD - POOL_HIST:POOL_PAD, :] = xbuf[:, POOL_PAD + L - POOL_HIST:POOL_PAD + L, :]


def _fused_kernel(xs_ref, ps_ref, ch_ref, ph_ref, xp_ref, pp_ref,
                  g_pre_ref, conv_w_ref, conv_b_ref, pool_scale_ref, g_post_ref,
                  w_in_hbm, w_grp_hbm, w_a_hbm, w_b_hbm, w_o_hbm, w_ple_hbm, w_pgate_hbm,
                  ys_ref, ncs_ref, nps_ref, yp_ref, ncp_ref, npp_ref,
                  w_in, w_grp, w_a, w_b, w_o, w_ple, w_pgate, stage, sem,
                  ubuf_s, xbuf_s, ubuf_p, xbuf_p, *, layer, sample_steps, prompt_tiles_per_seq):
    j = pl.program_id(0)
    hbm = dict(zip(MATS, (w_in_hbm, w_grp_hbm, w_a_hbm, w_b_hbm, w_o_hbm, w_ple_hbm, w_pgate_hbm)))
    mats = dict(zip(MATS, (w_in, w_grp, w_a, w_b, w_o, w_ple, w_pgate)))
    W = _Weights(g_pre_ref, conv_w_ref, conv_b_ref, pool_scale_ref, g_post_ref, mats)
    Ls, Lp = xs_ref.shape[1], xp_ref.shape[1]

    def sample_history():
        ubuf_s[:, CONV_PAD - (CONV_W - 1):CONV_PAD, :] = ch_ref[...]
        for k in range(POOL_HIST):
            xbuf_s[:, POOL_PAD - POOL_HIST + k, :] = ph_ref[k]

    def sample_tile(weights):
        _tile(xs_ref, ps_ref, ys_ref, ubuf_s, xbuf_s, weights, t=0, pos0=PAST_LEN,
              load_history=sample_history)
        ncs_ref[...] = ubuf_s[:, CONV_PAD + Ls - (CONV_W - 1):CONV_PAD + Ls, :]
        for k in range(POOL_HIST):
            nps_ref[k] = xbuf_s[:, POOL_PAD + Ls - POOL_HIST + k, :]

    @pl.when(j == 0)
    def _():
        stream = _WeightStream(layer, hbm, mats, stage, sem)
        stream.start()
        sample_tile(W._replace(stream=stream))
        stream.finish()

    @pl.when((j > 0) & (j < sample_steps))
    def _():
        sample_tile(W)

    @pl.when(j == sample_steps)
    def _():
        npp_ref[...] = jnp.zeros(npp_ref.shape, npp_ref.dtype)

    @pl.when(j >= sample_steps)
    def _():
        q = j - sample_steps
        b, t = q // prompt_tiles_per_seq, lax.rem(q, prompt_tiles_per_seq)
        _tile(xp_ref, pp_ref, yp_ref, ubuf_p, xbuf_p, W, t=t, pos0=t * Lp, load_history=None)
        ncp_ref[...] = ubuf_p[:, CONV_PAD + Lp - (CONV_W - 1):CONV_PAD + Lp, :]

        @pl.when(t == prompt_tiles_per_seq - 1)
        def _():
            row_shape = npp_ref.shape[1:]
            mine = lax.broadcasted_iota(jnp.int32, row_shape, 0) == b
            for k in range(POOL_HIST):
                row = xbuf_p[0, POOL_PAD + Lp - POOL_HIST + k:POOL_PAD + Lp - POOL_HIST + k + 1, :]
                npp_ref[k] = jnp.where(mine, jnp.broadcast_to(row, row_shape), npp_ref[k])


def _vmem_bytes(shape, dtype):
    itemsize = jnp.dtype(dtype).itemsize
    sublanes = SUBLANES * 4 // itemsize
    *lead, rows, cols = shape
    n = itemsize * (-(-rows // sublanes) * sublanes) * (-(-cols // LANES) * LANES)
    for s in lead:
        n *= s
    return n


def _layer(layer, xs, ps, conv_hist, pool_hist, xp, pp, small, big):
    f32, bf16 = jnp.float32, jnp.bfloat16
    Bs, Ts, D = xs.shape
    Bp, Tp, _ = xp.shape
    Ss, Ls = SAMPLE_TILE
    Sp, Lp = PROMPT_TILE
    assert Ls == Ts and Bs % Ss == 0 and Sp == 1 and Tp % Lp == 0 and min(Ls, Lp) >= POOL_HIST
    sample_steps = Bs // Ss
    tiles_per_seq = Tp // Lp
    grid = (sample_steps + Bp * tiles_per_seq,)

    def sample_block(j):
        return jnp.minimum(j, sample_steps - 1)

    def prompt_block(j):
        q = jnp.maximum(j - sample_steps, 0)
        return q // tiles_per_seq, q % tiles_per_seq

    def sample_rows(width):
        return pl.BlockSpec((Ss, Ls, width), lambda j: (sample_block(j), 0, 0))

    def prompt_rows(width):
        return pl.BlockSpec((Sp, Lp, width), lambda j: prompt_block(j) + (0,))

    def sample_conv(depth_index):
        return pl.BlockSpec((None, Ss, CONV_W - 1, D),
                            lambda j: (depth_index, sample_block(j), 0, 0))

    def sample_pool(depth_index):
        return pl.BlockSpec((None, POOL_HIST, Ss, D),
                            lambda j: (depth_index, 0, sample_block(j), 0))

    prompt_conv = pl.BlockSpec((None, Sp, CONV_W - 1, D), lambda j: (0, prompt_block(j)[0], 0, 0))
    prompt_pool = pl.BlockSpec((None, POOL_HIST, Bp, D), lambda j: (0, 0, 0, 0))

    def resident(arr):
        return pl.BlockSpec(arr.shape, lambda j: (0,) * arr.ndim, pipeline_mode=pl.Buffered(1))

    g_pre, conv_w, conv_b, pool_scale, g_post = small
    small_specs = [resident(g_pre),
                   pl.BlockSpec((None,) + conv_w.shape[1:], lambda j: (layer, 0, 0),
                                pipeline_mode=pl.Buffered(1)),
                   resident(conv_b), resident(pool_scale), resident(g_post)]
    in_specs = ([sample_rows(D), sample_rows(ps.shape[-1]), sample_conv(layer), sample_pool(layer),
                 prompt_rows(D), prompt_rows(pp.shape[-1])]
                + small_specs + [pl.BlockSpec(memory_space=pl.ANY)] * len(big))
    out_specs = [sample_rows(D), sample_conv(0), sample_pool(0),
                 prompt_rows(D), prompt_conv, prompt_pool]
    out_shape = [jax.ShapeDtypeStruct((Bs, Ts, D), f32),
                 jax.ShapeDtypeStruct((1, Bs, CONV_W - 1, D), f32),
                 jax.ShapeDtypeStruct((1, POOL_HIST, Bs, D), f32),
                 jax.ShapeDtypeStruct((Bp, Tp, D), f32),
                 jax.ShapeDtypeStruct((1, Bp, CONV_W - 1, D), f32),
                 jax.ShapeDtypeStruct((1, POOL_HIST, Bp, D), f32)]
    scratch = [(w.shape[1:], bf16) for w in big]
    scratch += [((STAGE_SLOTS, STAGE_ROWS, STAGE_COLS), f32)]
    buffers = [((Ss, CONV_PAD + Ls, D), f32), ((Ss, POOL_PAD + Ls, D), f32),
               ((Sp, CONV_PAD + Lp, D), f32), ((Sp, POOL_PAD + Lp, D), f32)]
    scratch_shapes = ([pltpu.VMEM(s, d) for s, d in scratch]
                      + [pltpu.SemaphoreType.DMA((STAGE_SLOTS,))]
                      + [pltpu.VMEM(s, d) for s, d in buffers])

    blocks = [((Ss, Ls, D), 2), ((Ss, Ls, ps.shape[-1]), 1), ((Ss, CONV_W - 1, D), 2),
              ((POOL_HIST, Ss, D), 2), ((Sp, Lp, D), 2), ((Sp, Lp, pp.shape[-1]), 1),
              ((Sp, CONV_W - 1, D), 1), ((POOL_HIST, Bp, D), 1)]
    vmem_estimate = (sum(_vmem_bytes(s, d) for s, d in scratch + buffers)
                     + sum(2 * n * _vmem_bytes(s, f32) for s, n in blocks)
                     + sum(_vmem_bytes(a.shape[-2:], f32) for a in small)
                     + min(2, Lp // CHAIN_ROWS) * LIVE_RESULTS_PER_CHAIN
                     * _vmem_bytes((CHAIN_ROWS, D), f32))
    assert vmem_estimate <= V7X_VMEM_BYTES, vmem_estimate

    return pl.pallas_call(
        functools.partial(_fused_kernel, layer=layer, sample_steps=sample_steps,
                          prompt_tiles_per_seq=tiles_per_seq),
        grid=grid,
        in_specs=in_specs,
        out_specs=out_specs,
        out_shape=out_shape,
        scratch_shapes=scratch_shapes,
        compiler_params=pltpu.CompilerParams(
            dimension_semantics=("arbitrary",),
            vmem_limit_bytes=vmem_estimate),
    )(xs, ps, conv_hist, pool_hist, xp, pp, *small, *big)


def kernel(x_prompt, x_sample, p_prompt, p_sample, cache_conv, state_pool, g_pre, w_in,
           conv_w, conv_b, w_grp, pool_scale, w_a_out, w_b_out, w_o, g_post, w_ple, w_pgate):
    depth = w_in.shape[0]
    hp, hs = x_prompt, x_sample
    big = (w_in, w_grp, w_a_out, w_b_out, w_o, w_ple, w_pgate)
    pool_hist = jnp.transpose(state_pool, STATE_TIME_MAJOR)
    conv_p, conv_s, pool_p, pool_s = [], [], [], []
    for i in range(depth):
        small = (g_pre[i][None, :], conv_w, conv_b[i][None, :], pool_scale[i][None, :],
                 g_post[i][None, :])
        hs, nc_s, np_s, hp, nc_p, np_p = _layer(i, hs, p_sample[i], cache_conv, pool_hist,
                                                hp, p_prompt[i], small, big)
        conv_p.append(nc_p); pool_p.append(jnp.transpose(np_p, STATE_TIME_MAJOR))
        conv_s.append(nc_s); pool_s.append(jnp.transpose(np_s, STATE_TIME_MAJOR))

    def stacked(parts):
        return parts[0] if depth == 1 else jnp.concatenate(parts, axis=0)

    return (hp, hs, stacked(conv_p), stacked(conv_s), stacked(pool_p), stacked(pool_s))
```

```python
import functools
from typing import Any, NamedTuple

import jax
import jax.numpy as jnp
from jax import lax
from jax.experimental import pallas as pl
from jax.experimental.pallas import tpu as pltpu

D_MODEL = 1024
CONV_W = 3
POOL_WINDOWS = (2, 4, 8, 16)
N_POOL_GROUPS = len(POOL_WINDOWS)
POOL_GW = D_MODEL // N_POOL_GROUPS
POOL_HIST = max(POOL_WINDOWS) - 1
PAST_LEN = 4096
EPS = 1e-6

SEG_XA, SEG_BA, SEG_CA, SEG_ZA, SEG_XB, SEG_ZB, SEG_GA, SEG_GB = range(8)

CONV_PAD = 8
POOL_PAD = 16
assert CONV_PAD >= CONV_W - 1 and POOL_PAD >= POOL_HIST

SUBLANES, LANES = 8, 128
V7X_VMEM_BYTES = 64 * 1024 * 1024
LIVE_RESULTS_PER_CHAIN = 3

PROMPT_TILE = (1, 512)
SAMPLE_TILE = (8, 32)
CHAIN_ROWS = 256

STAGE_ROWS, STAGE_COLS, STAGE_SLOTS = 256, 1024, 6

MATS = ("w_in", "w_grp", "w_a", "w_b", "w_o", "w_ple", "w_pgate")

STATE_TIME_MAJOR = (0, 2, 1, 3)


class _Weights(NamedTuple):
    g_pre: Any
    conv_w: Any
    conv_b: Any
    pool_scale: Any
    g_post: Any
    mats: Any
    stream: Any = None

    def dot(self, lhs, name, *, group=None, col0=0, ncols=None):
        ref = self.mats[name] if group is None else self.mats[name].at[group]
        ncols = ref.shape[1] if ncols is None else ncols
        cols = slice(col0, col0 + ncols)
        if self.stream is None:
            return jnp.dot(lhs, ref[:, cols], preferred_element_type=jnp.float32)
        acc = None
        for r0 in range(0, ref.shape[0], STAGE_ROWS):
            self.stream.need(name, group, r0, col0)
            part = jnp.dot(lhs[:, r0:r0 + STAGE_ROWS], ref[r0:r0 + STAGE_ROWS, cols],
                           preferred_element_type=jnp.float32)
            acc = part if acc is None else acc + part
        return acc


class _Unit(NamedTuple):
    name: str
    group: Any
    row0: int
    col0: int
    rows: int
    cols: int


def _stream_units(mats):
    D = D_MODEL

    def chunks(name, col0=0, group=None):
        rows, cols = mats[name].shape[-2:]
        cols = min(cols, STAGE_COLS)
        return [_Unit(name, group, r0, col0, min(rows, STAGE_ROWS), cols)
                for r0 in range(0, rows, STAGE_ROWS)]

    def seg(s):
        return chunks("w_in", s * D)

    units = seg(SEG_XB) + seg(SEG_CA) + seg(SEG_XA) + seg(SEG_BA) + seg(SEG_ZA) + seg(SEG_ZB)
    for g in range(N_POOL_GROUPS):
        units += chunks("w_grp", group=g)
    units += chunks("w_a") + seg(SEG_GA) + chunks("w_b") + seg(SEG_GB)
    units += chunks("w_ple") + chunks("w_o") + chunks("w_pgate")
    return units


class _WeightStream:
    AHEAD = 4

    def __init__(self, layer, hbm, mats, stage, sem):
        self.layer, self.hbm, self.mats, self.stage, self.sem = layer, hbm, mats, stage, sem
        self.units = _stream_units(mats)
        self.index = {u[:4]: i for i, u in enumerate(self.units)}
        self.arrived = 0

    def _copy(self, i):
        u = self.units[i]
        lead = (self.layer,) if u.group is None else (self.layer, u.group)
        src = self.hbm[u.name].at[lead + (pl.ds(u.row0, u.rows), pl.ds(u.col0, u.cols))]
        slot = i % STAGE_SLOTS
        return pltpu.make_async_copy(src, self.stage.at[slot, pl.ds(0, u.rows), pl.ds(0, u.cols)],
                                     self.sem.at[slot])

    def start(self):
        for i in range(min(STAGE_SLOTS, len(self.units))):
            self._copy(i).start()

    def _arrive(self):
        i = self.arrived
        u = self.units[i]
        self._copy(i).wait()
        dst = self.mats[u.name] if u.group is None else self.mats[u.name].at[u.group]
        dst[u.row0:u.row0 + u.rows, u.col0:u.col0 + u.cols] = (
            self.stage[i % STAGE_SLOTS, 0:u.rows, 0:u.cols].astype(dst.dtype))
        if i + STAGE_SLOTS < len(self.units):
            self._copy(i + STAGE_SLOTS).start()
        self.arrived += 1

    def need(self, name, group, row0, col0):
        i = self.index[(name, group, row0, col0)]
        assert i <= self.arrived, "weights are read in another order than they are streamed"
        if self.arrived <= min(i + 1, len(self.units) - 1):
            while self.arrived < min(i + 1 + self.AHEAD, len(self.units)):
                self._arrive()

    def finish(self):
        while self.arrived < len(self.units):
            self._arrive()


def _rmsnorm(x, g):
    ms = jnp.mean(x * x, axis=-1, keepdims=True)
    return x * lax.rsqrt(ms + EPS) * g


def _sigmoid(x):
    return 0.5 * jnp.tanh(0.5 * x) + 0.5


def _silu(x):
    half = 0.5 * x
    return half * jnp.tanh(half) + half


def _shift_rows(v, k):
    return pltpu.roll(v, k, axis=v.ndim - 2)


def _trailing_sum(v, w):
    assert w & (w - 1) == 0
    k = 1
    while k < w:
        v = v + _shift_rows(v, k)
        k *= 2
    return v


_TAIL = "tail"
_STAY = "stay"


def _chain(s0, r0, Ss, R, pos0, x_ref, p_ref, y_ref, ubuf, xbuf, W, embed_first):
    D = x_ref.shape[-1]
    M = Ss * R
    f32, bf16 = jnp.float32, jnp.bfloat16
    streams = slice(s0, s0 + Ss)

    def embed():
        p = p_ref[streams, r0:r0 + R, :].reshape(M, p_ref.shape[-1])
        return W.dot(p.astype(bf16), "w_ple")

    if embed_first:
        pe = embed()
        yield _STAY
    x = x_ref[streams, r0:r0 + R, :].reshape(M, D)
    h = _rmsnorm(x, W.g_pre[...]).astype(bf16)
    yield

    def proj(seg):
        return W.dot(h, "w_in", col0=seg * D, ncols=D)

    xb = proj(SEG_XB).reshape(Ss, R, D)
    yield
    xbuf[streams, POOL_PAD + r0:POOL_PAD + r0 + R, :] = xb
    ca = proj(SEG_CA)
    yield
    u = (ca * proj(SEG_XA)).reshape(Ss, R, D)
    yield
    ubuf[streams, CONV_PAD + r0:CONV_PAD + r0 + R, :] = u

    pos = lax.broadcasted_iota(jnp.int32, (Ss, R, POOL_GW), 1) + (pos0 + r0)
    d = []
    for g, w in enumerate(POOL_WINDOWS):
        cols = slice(g * POOL_GW, (g + 1) * POOL_GW)
        ext = xbuf[streams, r0:r0 + POOL_PAD + R, cols]
        xg = ext[:, POOL_PAD:, :]
        s = _trailing_sum(ext, w)[:, POOL_PAD:, :]
        cnt = jnp.minimum(pos + 1, w).astype(f32)
        d.append((s / cnt - xg).reshape(M, POOL_GW).astype(bf16))

    ext = ubuf[streams, r0:r0 + CONV_PAD + R, :]
    conv = W.conv_b[...] + W.conv_w[CONV_W - 1:CONV_W, :] * ext[:, CONV_PAD:, :]
    for k in range(CONV_W - 1):
        shifted = _shift_rows(ext, CONV_W - 1 - k)[:, CONV_PAD:, :]
        conv = conv + W.conv_w[k:k + 1, :] * shifted
    ba = proj(SEG_BA)
    yield
    a = ((ba * conv.reshape(M, D)) * _silu(proj(SEG_ZA))).astype(bf16)
    yield
    zb = proj(SEG_ZB)
    yield
    mixed = []
    for g in range(N_POOL_GROUPS):
        mixed.append(W.dot(d[g], "w_grp", group=g))
        yield
    mixed = jnp.concatenate(mixed, axis=-1) * W.pool_scale[...]
    b = (mixed * _silu(zb)).astype(bf16)
    ya = W.dot(a, "w_a")
    yield
    m = _sigmoid(proj(SEG_GA)) * ya
    yield
    yb = W.dot(b, "w_b")
    yield _TAIL
    m = (m + _sigmoid(proj(SEG_GB)) * yb).astype(bf16)
    yield
    if not embed_first:
        pe = embed()
        yield

    n_parts = 1 if embed_first else 2
    parts = [(k * M // n_parts, (k + 1) * M // n_parts) for k in range(n_parts)]
    o = []
    for lo, hi in parts:
        o.append(W.dot(m[lo:hi], "w_o"))
        yield
    for (lo, hi), o_part in zip(parts, o):
        x1 = x[lo:hi] + _rmsnorm(o_part, W.g_post[...])
        gate = W.dot(x1.astype(bf16), "w_pgate")
        yield
        y = x1 + pe[lo:hi] * _sigmoid(gate)
        if Ss > 1:
            y_ref[s0 + lo // R:s0 + hi // R, r0:r0 + R, :] = y.reshape((hi - lo) // R, R, D)
        else:
            y_ref[streams, r0 + lo:r0 + hi, :] = y.reshape(1, hi - lo, D)


def _run_interleaved(chains):
    done = object()
    prev = None
    for cur in chains:
        for marker in cur:
            if marker is _STAY:
                continue
            if prev is not None and next(prev, done) is done:
                prev = None
            if marker is _TAIL:
                break
        if prev is not None:
            for _ in prev:
                pass
        prev = cur
    for _ in prev:
        pass


def _chain_shape(S, L):
    if L >= CHAIN_ROWS:
        assert L % CHAIN_ROWS == 0
        return 1, CHAIN_ROWS
    assert CHAIN_ROWS % L == 0
    return min(S, CHAIN_ROWS // L), L


def _tile(x_ref, p_ref, y_ref, ubuf, xbuf, W, *, t, pos0, load_history):
    S, L, _ = x_ref.shape
    Ss, R = _chain_shape(S, L)

    @pl.when(t == 0)
    def _():
        ubuf[:, 0:CONV_PAD, :] = jnp.zeros((S, CONV_PAD, ubuf.shape[-1]), ubuf.dtype)
        xbuf[:, 0:POOL_PAD, :] = jnp.zeros((S, POOL_PAD, xbuf.shape[-1]), xbuf.dtype)
        if load_history is not None:
            load_history()

    starts = [(s0, r0) for s0 in range(0, S, Ss) for r0 in range(0, L, R)]
    assert W.stream is None or len(starts) == 1
    _run_interleaved(_chain(s0, r0, Ss, R, pos0, x_ref, p_ref, y_ref, ubuf, xbuf, W,
                            embed_first=i < len(starts) - 1)
                     for i, (s0, r0) in enumerate(starts))

    ubuf[:, CONV_PAD - (CONV_W - 1):CONV_PAD, :] = ubuf[:, CONV_PAD + L - (CONV_W - 1):CONV_PAD + L, :]
    xbuf[:, POOL_PAD - POOL_HIST:POOL_PAD, :] = xbuf[:, POOL_PAD + L - POOL_HIST:POOL_PAD + L, :]


def _fused_kernel(xs_ref, ps_ref, ch_ref, ph_ref, xp_ref, pp_ref,
                  g_pre_ref, conv_w_ref, conv_b_ref, pool_scale_ref, g_post_ref,
                  w_in_hbm, w_grp_hbm, w_a_hbm, w_b_hbm, w_o_hbm, w_ple_hbm, w_pgate_hbm,
                  ys_ref, ncs_ref, nps_ref, yp_ref, ncp_ref, npp_ref,
                  w_in, w_grp, w_a, w_b, w_o, w_ple, w_pgate, stage, sem,
                  ubuf_s, xbuf_s, ubuf_p, xbuf_p, *, layer, sample_steps, prompt_tiles_per_seq):
    j = pl.program_id(0)
    hbm = dict(zip(MATS, (w_in_hbm, w_grp_hbm, w_a_hbm, w_b_hbm, w_o_hbm, w_ple_hbm, w_pgate_hbm)))
    mats = dict(zip(MATS, (w_in, w_grp, w_a, w_b, w_o, w_ple, w_pgate)))
    W = _Weights(g_pre_ref, conv_w_ref, conv_b_ref, pool_scale_ref, g_post_ref, mats)
    Ls, Lp = xs_ref.shape[1], xp_ref.shape[1]

    def sample_history():
        ubuf_s[:, CONV_PAD - (CONV_W - 1):CONV_PAD, :] = ch_ref[...]
        for k in range(POOL_HIST):
            xbuf_s[:, POOL_PAD - POOL_HIST + k, :] = ph_ref[k]

    def sample_tile(weights):
        _tile(xs_ref, ps_ref, ys_ref, ubuf_s, xbuf_s, weights, t=0, pos0=PAST_LEN,
              load_history=sample_history)
        ncs_ref[...] = ubuf_s[:, CONV_PAD + Ls - (CONV_W - 1):CONV_PAD + Ls, :]
        for k in range(POOL_HIST):
            nps_ref[k] = xbuf_s[:, POOL_PAD + Ls - POOL_HIST + k, :]

    @pl.when(j == 0)
    def _():
        stream = _WeightStream(layer, hbm, mats, stage, sem)
        stream.start()
        sample_tile(W._replace(stream=stream))
        stream.finish()

    @pl.when((j > 0) & (j < sample_steps))
    def _():
        sample_tile(W)

    @pl.when(j == sample_steps)
    def _():
        npp_ref[...] = jnp.zeros(npp_ref.shape, npp_ref.dtype)

    @pl.when(j >= sample_steps)
    def _():
        q = j - sample_steps
        b, t = q // prompt_tiles_per_seq, lax.rem(q, prompt_tiles_per_seq)
        _tile(xp_ref, pp_ref, yp_ref, ubuf_p, xbuf_p, W, t=t, pos0=t * Lp, load_history=None)
        ncp_ref[...] = ubuf_p[:, CONV_PAD + Lp - (CONV_W - 1):CONV_PAD + Lp, :]

        @pl.when(t == prompt_tiles_per_seq - 1)
        def _():
            row_shape = npp_ref.shape[1:]
            mine = lax.broadcasted_iota(jnp.int32, row_shape, 0) == b
            for k in range(POOL_HIST):
                row = xbuf_p[0, POOL_PAD + Lp - POOL_HIST + k:POOL_PAD + Lp - POOL_HIST + k + 1, :]
                npp_ref[k] = jnp.where(mine, jnp.broadcast_to(row, row_shape), npp_ref[k])


def _vmem_bytes(shape, dtype):
    itemsize = jnp.dtype(dtype).itemsize
    sublanes = SUBLANES * 4 // itemsize
    *lead, rows, cols = shape
    n = itemsize * (-(-rows // sublanes) * sublanes) * (-(-cols // LANES) * LANES)
    for s in lead:
        n *= s
    return n


def _layer(layer, xs, ps, conv_hist, pool_hist, xp, pp, small, big):
    f32, bf16 = jnp.float32, jnp.bfloat16
    Bs, Ts, D = xs.shape
    Bp, Tp, _ = xp.shape
    Ss, Ls = SAMPLE_TILE
    Sp, Lp = PROMPT_TILE
    assert Ls == Ts and Bs % Ss == 0 and Sp == 1 and Tp % Lp == 0 and min(Ls, Lp) >= POOL_HIST
    sample_steps = Bs // Ss
    tiles_per_seq = Tp // Lp
    grid = (sample_steps + Bp * tiles_per_seq,)

    def sample_block(j):
        return jnp.minimum(j, sample_steps - 1)

    def prompt_block(j):
        q = jnp.maximum(j - sample_steps, 0)
        return q // tiles_per_seq, q % tiles_per_seq

    def sample_rows(width):
        return pl.BlockSpec((Ss, Ls, width), lambda j: (sample_block(j), 0, 0))

    def prompt_rows(width):
        return pl.BlockSpec((Sp, Lp, width), lambda j: prompt_block(j) + (0,))

    def sample_conv(depth_index):
        return pl.BlockSpec((None, Ss, CONV_W - 1, D),
                            lambda j: (depth_index, sample_block(j), 0, 0))

    def sample_pool(depth_index):
        return pl.BlockSpec((None, POOL_HIST, Ss, D),
                            lambda j: (depth_index, 0, sample_block(j), 0))

    prompt_conv = pl.BlockSpec((None, Sp, CONV_W - 1, D), lambda j: (0, prompt_block(j)[0], 0, 0))
    prompt_pool = pl.BlockSpec((None, POOL_HIST, Bp, D), lambda j: (0, 0, 0, 0))

    def resident(arr):
        return pl.BlockSpec(arr.shape, lambda j: (0,) * arr.ndim, pipeline_mode=pl.Buffered(1))

    g_pre, conv_w, conv_b, pool_scale, g_post = small
    small_specs = [resident(g_pre),
                   pl.BlockSpec((None,) + conv_w.shape[1:], lambda j: (layer, 0, 0),
                                pipeline_mode=pl.Buffered(1)),
                   resident(conv_b), resident(pool_scale), resident(g_post)]
    in_specs = ([sample_rows(D), sample_rows(ps.shape[-1]), sample_conv(layer), sample_pool(layer),
                 prompt_rows(D), prompt_rows(pp.shape[-1])]
                + small_specs + [pl.BlockSpec(memory_space=pl.ANY)] * len(big))
    out_specs = [sample_rows(D), sample_conv(0), sample_pool(0),
                 prompt_rows(D), prompt_conv, prompt_pool]
    out_shape = [jax.ShapeDtypeStruct((Bs, Ts, D), f32),
                 jax.ShapeDtypeStruct((1, Bs, CONV_W - 1, D), f32),
                 jax.ShapeDtypeStruct((1, POOL_HIST, Bs, D), f32),
                 jax.ShapeDtypeStruct((Bp, Tp, D), f32),
                 jax.ShapeDtypeStruct((1, Bp, CONV_W - 1, D), f32),
                 jax.ShapeDtypeStruct((1, POOL_HIST, Bp, D), f32)]
    scratch = [(w.shape[1:], bf16) for w in big]
    scratch += [((STAGE_SLOTS, STAGE_ROWS, STAGE_COLS), f32)]
    buffers = [((Ss, CONV_PAD + Ls, D), f32), ((Ss, POOL_PAD + Ls, D), f32),
               ((Sp, CONV_PAD + Lp, D), f32), ((Sp, POOL_PAD + Lp, D), f32)]
    scratch_shapes = ([pltpu.VMEM(s, d) for s, d in scratch]
                      + [pltpu.SemaphoreType.DMA((STAGE_SLOTS,))]
                      + [pltpu.VMEM(s, d) for s, d in buffers])

    blocks = [((Ss, Ls, D), 2), ((Ss, Ls, ps.shape[-1]), 1), ((Ss, CONV_W - 1, D), 2),
              ((POOL_HIST, Ss, D), 2), ((Sp, Lp, D), 2), ((Sp, Lp, pp.shape[-1]), 1),
              ((Sp, CONV_W - 1, D), 1), ((POOL_HIST, Bp, D), 1)]
    vmem_estimate = (sum(_vmem_bytes(s, d) for s, d in scratch + buffers)
                     + sum(2 * n * _vmem_bytes(s, f32) for s, n in blocks)
                     + sum(_vmem_bytes(a.shape[-2:], f32) for a in small)
                     + min(2, Lp // CHAIN_ROWS) * LIVE_RESULTS_PER_CHAIN
                     * _vmem_bytes((CHAIN_ROWS, D), f32))
    assert vmem_estimate <= V7X_VMEM_BYTES, vmem_estimate

    return pl.pallas_call(
        functools.partial(_fused_kernel, layer=layer, sample_steps=sample_steps,
                          prompt_tiles_per_seq=tiles_per_seq),
        grid=grid,
        in_specs=in_specs,
        out_specs=out_specs,
        out_shape=out_shape,
        scratch_shapes=scratch_shapes,
        compiler_params=pltpu.CompilerParams(
            dimension_semantics=("arbitrary",),
            vmem_limit_bytes=vmem_estimate),
    )(xs, ps, conv_hist, pool_hist, xp, pp, *small, *big)


def kernel(x_prompt, x_sample, p_prompt, p_sample, cache_conv, state_pool, g_pre, w_in,
           conv_w, conv_b, w_grp, pool_scale, w_a_out, w_b_out, w_o, g_post, w_ple, w_pgate):
    depth = w_in.shape[0]
    hp, hs = x_prompt, x_sample
    big = (w_in, w_grp, w_a_out, w_b_out, w_o, w_ple, w_pgate)
    pool_hist = jnp.transpose(state_pool, STATE_TIME_MAJOR)
    conv_p, conv_s, pool_p, pool_s = [], [], [], []
    for i in range(depth):
        small = (g_pre[i][None, :], conv_w, conv_b[i][None, :], pool_scale[i][None, :],
                 g_post[i][None, :])
        hs, nc_s, np_s, hp, nc_p, np_p = _layer(i, hs, p_sample[i], cache_conv, pool_hist,
                                                hp, p_prompt[i], small, big)
        conv_p.append(nc_p); pool_p.append(jnp.transpose(np_p, STATE_TIME_MAJOR))
        conv_s.append(nc_s); pool_s.append(jnp.transpose(np_s, STATE_TIME_MAJOR))

    def stacked(parts):
        return parts[0] if depth == 1 else jnp.concatenate(parts, axis=0)

    return (hp, hs, stacked(conv_p), stacked(conv_s), stacked(pool_p), stacked(pool_s))
```

```python
import functools
from typing import Any, NamedTuple

import jax
import jax.numpy as jnp
from jax import lax
from jax.experimental import pallas as pl
from jax.experimental.pallas import tpu as pltpu

D_MODEL = 1024
CONV_W = 3
POOL_WINDOWS = (2, 4, 8, 16)
N_POOL_GROUPS = len(POOL_WINDOWS)
POOL_GW = D_MODEL // N_POOL_GROUPS
POOL_HIST = max(POOL_WINDOWS) - 1
PAST_LEN = 4096
EPS = 1e-6

SEG_XA, SEG_BA, SEG_CA, SEG_ZA, SEG_XB, SEG_ZB, SEG_GA, SEG_GB = range(8)

CONV_PAD = 8
POOL_PAD = 16
assert CONV_PAD >= CONV_W - 1 and POOL_PAD >= POOL_HIST

SUBLANES, LANES = 8, 128
V7X_VMEM_BYTES = 64 * 1024 * 1024
LIVE_RESULTS_PER_CHAIN = 3

PROMPT_TILE = (2, 256)
SAMPLE_TILE = (8, 32)
CHAIN_ROWS = 256

STAGE_ROWS, STAGE_COLS, STAGE_SLOTS = 256, 1024, 6

MATS = ("w_in", "w_grp", "w_a", "w_b", "w_o", "w_ple", "w_pgate")

STATE_TIME_MAJOR = (0, 2, 1, 3)


class _Weights(NamedTuple):
    g_pre: Any
    conv_w: Any
    conv_b: Any
    pool_scale: Any
    g_post: Any
    mats: Any
    stream: Any = None

    def dot(self, lhs, name, *, group=None, col0=0, ncols=None):
        ref = self.mats[name] if group is None else self.mats[name].at[group]
        ncols = ref.shape[1] if ncols is None else ncols
        cols = slice(col0, col0 + ncols)
        if self.stream is None:
            return jnp.dot(lhs, ref[:, cols], preferred_element_type=jnp.float32)
        acc = None
        for r0 in range(0, ref.shape[0], STAGE_ROWS):
            self.stream.need(name, group, r0, col0)
            part = jnp.dot(lhs[:, r0:r0 + STAGE_ROWS], ref[r0:r0 + STAGE_ROWS, cols],
                           preferred_element_type=jnp.float32)
            acc = part if acc is None else acc + part
        return acc


class _Unit(NamedTuple):
    name: str
    group: Any
    row0: int
    col0: int
    rows: int
    cols: int


def _stream_units(mats):
    D = D_MODEL

    def chunks(name, col0=0, group=None):
        rows, cols = mats[name].shape[-2:]
        cols = min(cols, STAGE_COLS)
        return [_Unit(name, group, r0, col0, min(rows, STAGE_ROWS), cols)
                for r0 in range(0, rows, STAGE_ROWS)]

    def seg(s):
        return chunks("w_in", s * D)

    units = seg(SEG_XB) + seg(SEG_CA) + seg(SEG_XA) + seg(SEG_BA) + seg(SEG_ZA) + seg(SEG_ZB)
    for g in range(N_POOL_GROUPS):
        units += chunks("w_grp", group=g)
    units += chunks("w_a") + seg(SEG_GA) + chunks("w_b") + seg(SEG_GB)
    units += chunks("w_ple") + chunks("w_o") + chunks("w_pgate")
    return units


class _WeightStream:
    AHEAD = 4

    def __init__(self, layer, hbm, mats, stage, sem):
        self.layer, self.hbm, self.mats, self.stage, self.sem = layer, hbm, mats, stage, sem
        self.units = _stream_units(mats)
        self.index = {u[:4]: i for i, u in enumerate(self.units)}
        self.arrived = 0

    def _copy(self, i):
        u = self.units[i]
        lead = (self.layer,) if u.group is None else (self.layer, u.group)
        src = self.hbm[u.name].at[lead + (pl.ds(u.row0, u.rows), pl.ds(u.col0, u.cols))]
        slot = i % STAGE_SLOTS
        return pltpu.make_async_copy(src, self.stage.at[slot, pl.ds(0, u.rows), pl.ds(0, u.cols)],
                                     self.sem.at[slot])

    def start(self):
        for i in range(min(STAGE_SLOTS, len(self.units))):
            self._copy(i).start()

    def _arrive(self):
        i = self.arrived
        u = self.units[i]
        self._copy(i).wait()
        dst = self.mats[u.name] if u.group is None else self.mats[u.name].at[u.group]
        dst[u.row0:u.row0 + u.rows, u.col0:u.col0 + u.cols] = (
            self.stage[i % STAGE_SLOTS, 0:u.rows, 0:u.cols].astype(dst.dtype))
        if i + STAGE_SLOTS < len(self.units):
            self._copy(i + STAGE_SLOTS).start()
        self.arrived += 1

    def need(self, name, group, row0, col0):
        i = self.index[(name, group, row0, col0)]
        assert i <= self.arrived, "weights are read in another order than they are streamed"
        if self.arrived <= min(i + 1, len(self.units) - 1):
            while self.arrived < min(i + 1 + self.AHEAD, len(self.units)):
                self._arrive()

    def finish(self):
        while self.arrived < len(self.units):
            self._arrive()


def _rmsnorm(x, g):
    ms = jnp.mean(x * x, axis=-1, keepdims=True)
    return x * lax.rsqrt(ms + EPS) * g


def _sigmoid(x):
    return 0.5 * jnp.tanh(0.5 * x) + 0.5


def _silu(x):
    half = 0.5 * x
    return half * jnp.tanh(half) + half


def _shift_rows(v, k):
    return pltpu.roll(v, k, axis=v.ndim - 2)


def _trailing_sum(v, w):
    assert w & (w - 1) == 0
    k = 1
    while k < w:
        v = v + _shift_rows(v, k)
        k *= 2
    return v


_TAIL = "tail"
_STAY = "stay"


def _chain(s0, r0, Ss, R, pos0, x_ref, p_ref, y_ref, ubuf, xbuf, W, embed_first):
    D = x_ref.shape[-1]
    M = Ss * R
    f32, bf16 = jnp.float32, jnp.bfloat16
    streams = slice(s0, s0 + Ss)

    def embed():
        p = p_ref[streams, r0:r0 + R, :].reshape(M, p_ref.shape[-1])
        return W.dot(p.astype(bf16), "w_ple")

    if embed_first:
        pe = embed()
        yield _STAY
    x = x_ref[streams, r0:r0 + R, :].reshape(M, D)
    h = _rmsnorm(x, W.g_pre[...]).astype(bf16)
    yield

    def proj(seg):
        return W.dot(h, "w_in", col0=seg * D, ncols=D)

    xb = proj(SEG_XB).reshape(Ss, R, D)
    yield
    xbuf[streams, POOL_PAD + r0:POOL_PAD + r0 + R, :] = xb
    ca = proj(SEG_CA)
    yield
    u = (ca * proj(SEG_XA)).reshape(Ss, R, D)
    yield
    ubuf[streams, CONV_PAD + r0:CONV_PAD + r0 + R, :] = u

    pos = lax.broadcasted_iota(jnp.int32, (Ss, R, POOL_GW), 1) + (pos0 + r0)
    d = []
    for g, w in enumerate(POOL_WINDOWS):
        cols = slice(g * POOL_GW, (g + 1) * POOL_GW)
        ext = xbuf[streams, r0:r0 + POOL_PAD + R, cols]
        xg = ext[:, POOL_PAD:, :]
        s = _trailing_sum(ext, w)[:, POOL_PAD:, :]
        cnt = jnp.minimum(pos + 1, w).astype(f32)
        d.append((s / cnt - xg).reshape(M, POOL_GW).astype(bf16))

    ext = ubuf[streams, r0:r0 + CONV_PAD + R, :]
    conv = W.conv_b[...] + W.conv_w[CONV_W - 1:CONV_W, :] * ext[:, CONV_PAD:, :]
    for k in range(CONV_W - 1):
        shifted = _shift_rows(ext, CONV_W - 1 - k)[:, CONV_PAD:, :]
        conv = conv + W.conv_w[k:k + 1, :] * shifted
    ba = proj(SEG_BA)
    yield
    a = ((ba * conv.reshape(M, D)) * _silu(proj(SEG_ZA))).astype(bf16)
    yield
    zb = proj(SEG_ZB)
    yield
    mixed = []
    for g in range(N_POOL_GROUPS):
        mixed.append(W.dot(d[g], "w_grp", group=g))
        yield
    mixed = jnp.concatenate(mixed, axis=-1) * W.pool_scale[...]
    b = (mixed * _silu(zb)).astype(bf16)
    ya = W.dot(a, "w_a")
    yield
    m = _sigmoid(proj(SEG_GA)) * ya
    yield
    yb = W.dot(b, "w_b")
    yield _TAIL
    m = (m + _sigmoid(proj(SEG_GB)) * yb).astype(bf16)
    yield
    if not embed_first:
        pe = embed()
        yield

    n_parts = 1 if embed_first else 2
    parts = [(k * M // n_parts, (k + 1) * M // n_parts) for k in range(n_parts)]
    o = []
    for lo, hi in parts:
        o.append(W.dot(m[lo:hi], "w_o"))
        yield
    for (lo, hi), o_part in zip(parts, o):
        x1 = x[lo:hi] + _rmsnorm(o_part, W.g_post[...])
        gate = W.dot(x1.astype(bf16), "w_pgate")
        yield
        y = x1 + pe[lo:hi] * _sigmoid(gate)
        if Ss > 1:
            y_ref[s0 + lo // R:s0 + hi // R, r0:r0 + R, :] = y.reshape((hi - lo) // R, R, D)
        else:
            y_ref[streams, r0 + lo:r0 + hi, :] = y.reshape(1, hi - lo, D)


def _run_interleaved(chains):
    done = object()
    prev = None
    for cur in chains:
        for marker in cur:
            if marker is _STAY:
                continue
            if prev is not None and next(prev, done) is done:
                prev = None
            if marker is _TAIL:
                break
        if prev is not None:
            for _ in prev:
                pass
        prev = cur
    for _ in prev:
        pass


def _chain_shape(S, L):
    if L >= CHAIN_ROWS:
        assert L % CHAIN_ROWS == 0
        return 1, CHAIN_ROWS
    assert CHAIN_ROWS % L == 0
    return min(S, CHAIN_ROWS // L), L


def _tile(x_ref, p_ref, y_ref, ubuf, xbuf, W, *, t, pos0, load_history):
    S, L, _ = x_ref.shape
    Ss, R = _chain_shape(S, L)

    @pl.when(t == 0)
    def _():
        ubuf[:, 0:CONV_PAD, :] = jnp.zeros((S, CONV_PAD, ubuf.shape[-1]), ubuf.dtype)
        xbuf[:, 0:POOL_PAD, :] = jnp.zeros((S, POOL_PAD, xbuf.shape[-1]), xbuf.dtype)
        if load_history is not None:
            load_history()

    starts = [(s0, r0) for s0 in range(0, S, Ss) for r0 in range(0, L, R)]
    assert W.stream is None or len(starts) == 1
    _run_interleaved(_chain(s0, r0, Ss, R, pos0, x_ref, p_ref, y_ref, ubuf, xbuf, W,
                            embed_first=i < len(starts) - 1)
                     for i, (s0, r0) in enumerate(starts))

    ubuf[:, CONV_PAD - (CONV_W - 1):CONV_PAD, :] = ubuf[:, CONV_PAD + L - (CONV_W - 1):CONV_PAD + L, :]
    xbuf[:, POOL_PAD - POOL_HIST:POOL_PAD, :] = xbuf[:, POOL_PAD + L - POOL_HIST:POOL_PAD + L, :]


def _fused_kernel(xs_ref, ps_ref, ch_ref, ph_ref, xp_ref, pp_ref,
                  g_pre_ref, conv_w_ref, conv_b_ref, pool_scale_ref, g_post_ref,
                  w_in_hbm, w_grp_hbm, w_a_hbm, w_b_hbm, w_o_hbm, w_ple_hbm, w_pgate_hbm,
                  ys_ref, ncs_ref, nps_ref, yp_ref, ncp_ref, npp_ref,
                  w_in, w_grp, w_a, w_b, w_o, w_ple, w_pgate, stage, sem,
                  ubuf_s, xbuf_s, ubuf_p, xbuf_p, *, layer, sample_steps, prompt_tiles_per_seq):
    j = pl.program_id(0)
    hbm = dict(zip(MATS, (w_in_hbm, w_grp_hbm, w_a_hbm, w_b_hbm, w_o_hbm, w_ple_hbm, w_pgate_hbm)))
    mats = dict(zip(MATS, (w_in, w_grp, w_a, w_b, w_o, w_ple, w_pgate)))
    W = _Weights(g_pre_ref, conv_w_ref, conv_b_ref, pool_scale_ref, g_post_ref, mats)
    Ls, Lp = xs_ref.shape[1], xp_ref.shape[1]

    def sample_history():
        ubuf_s[:, CONV_PAD - (CONV_W - 1):CONV_PAD, :] = ch_ref[...]
        for k in range(POOL_HIST):
            xbuf_s[:, POOL_PAD - POOL_HIST + k, :] = ph_ref[k]

    def sample_tile(weights):
        _tile(xs_ref, ps_ref, ys_ref, ubuf_s, xbuf_s, weights, t=0, pos0=PAST_LEN,
              load_history=sample_history)
        ncs_ref[...] = ubuf_s[:, CONV_PAD + Ls - (CONV_W - 1):CONV_PAD + Ls, :]
        for k in range(POOL_HIST):
            nps_ref[k] = xbuf_s[:, POOL_PAD + Ls - POOL_HIST + k, :]

    @pl.when(j == 0)
    def _():
        stream = _WeightStream(layer, hbm, mats, stage, sem)
        stream.start()
        sample_tile(W._replace(stream=stream))
        stream.finish()

    @pl.when((j > 0) & (j < sample_steps))
    def _():
        sample_tile(W)

    @pl.when(j == sample_steps)
    def _():
        npp_ref[...] = jnp.zeros(npp_ref.shape, npp_ref.dtype)

    @pl.when(j >= sample_steps)
    def _():
        q = j - sample_steps
        b, t = q // prompt_tiles_per_seq, lax.rem(q, prompt_tiles_per_seq)
        _tile(xp_ref, pp_ref, yp_ref, ubuf_p, xbuf_p, W, t=t, pos0=t * Lp, load_history=None)
        ncp_ref[...] = ubuf_p[:, CONV_PAD + Lp - (CONV_W - 1):CONV_PAD + Lp, :]

        @pl.when(t == prompt_tiles_per_seq - 1)
        def _():
            row_shape = npp_ref.shape[1:]
            seq = lax.broadcasted_iota(jnp.int32, row_shape, 0)
            Sp = xp_ref.shape[0]
            for s in range(Sp):
                mine = seq == b * Sp + s
                for k in range(POOL_HIST):
                    lo = POOL_PAD + Lp - POOL_HIST + k
                    row = xbuf_p[s, lo:lo + 1, :]
                    npp_ref[k] = jnp.where(mine, jnp.broadcast_to(row, row_shape), npp_ref[k])


def _vmem_bytes(shape, dtype):
    itemsize = jnp.dtype(dtype).itemsize
    sublanes = SUBLANES * 4 // itemsize
    *lead, rows, cols = shape
    n = itemsize * (-(-rows // sublanes) * sublanes) * (-(-cols // LANES) * LANES)
    for s in lead:
        n *= s
    return n


def _layer(layer, xs, ps, conv_hist, pool_hist, xp, pp, small, big):
    f32, bf16 = jnp.float32, jnp.bfloat16
    Bs, Ts, D = xs.shape
    Bp, Tp, _ = xp.shape
    Ss, Ls = SAMPLE_TILE
    Sp, Lp = PROMPT_TILE
    assert Ls == Ts and Bs % Ss == 0 and Bp % Sp == 0 and Tp % Lp == 0 and min(Ls, Lp) >= POOL_HIST
    sample_steps = Bs // Ss
    tiles_per_seq = Tp // Lp
    grid = (sample_steps + Bp // Sp * tiles_per_seq,)

    def sample_block(j):
        return jnp.minimum(j, sample_steps - 1)

    def prompt_block(j):
        q = jnp.maximum(j - sample_steps, 0)
        return q // tiles_per_seq, q % tiles_per_seq

    def sample_rows(width):
        return pl.BlockSpec((Ss, Ls, width), lambda j: (sample_block(j), 0, 0))

    def prompt_rows(width):
        return pl.BlockSpec((Sp, Lp, width), lambda j: prompt_block(j) + (0,))

    def sample_conv(depth_index):
        return pl.BlockSpec((None, Ss, CONV_W - 1, D),
                            lambda j: (depth_index, sample_block(j), 0, 0))

    def sample_pool(depth_index):
        return pl.BlockSpec((None, POOL_HIST, Ss, D),
                            lambda j: (depth_index, 0, sample_block(j), 0))

    prompt_conv = pl.BlockSpec((None, Sp, CONV_W - 1, D), lambda j: (0, prompt_block(j)[0], 0, 0))
    prompt_pool = pl.BlockSpec((None, POOL_HIST, Bp, D), lambda j: (0, 0, 0, 0))

    def resident(arr):
        return pl.BlockSpec(arr.shape, lambda j: (0,) * arr.ndim, pipeline_mode=pl.Buffered(1))

    g_pre, conv_w, conv_b, pool_scale, g_post = small
    small_specs = [resident(g_pre),
                   pl.BlockSpec((None,) + conv_w.shape[1:], lambda j: (layer, 0, 0),
                                pipeline_mode=pl.Buffered(1)),
                   resident(conv_b), resident(pool_scale), resident(g_post)]
    in_specs = ([sample_rows(D), sample_rows(ps.shape[-1]), sample_conv(layer), sample_pool(layer),
                 prompt_rows(D), prompt_rows(pp.shape[-1])]
                + small_specs + [pl.BlockSpec(memory_space=pl.ANY)] * len(big))
    out_specs = [sample_rows(D), sample_conv(0), sample_pool(0),
                 prompt_rows(D), prompt_conv, prompt_pool]
    out_shape = [jax.ShapeDtypeStruct((Bs, Ts, D), f32),
                 jax.ShapeDtypeStruct((1, Bs, CONV_W - 1, D), f32),
                 jax.ShapeDtypeStruct((1, POOL_HIST, Bs, D), f32),
                 jax.ShapeDtypeStruct((Bp, Tp, D), f32),
                 jax.ShapeDtypeStruct((1, Bp, CONV_W - 1, D), f32),
                 jax.ShapeDtypeStruct((1, POOL_HIST, Bp, D), f32)]
    scratch = [(w.shape[1:], bf16) for w in big]
    scratch += [((STAGE_SLOTS, STAGE_ROWS, STAGE_COLS), f32)]
    buffers = [((Ss, CONV_PAD + Ls, D), f32), ((Ss, POOL_PAD + Ls, D), f32),
               ((Sp, CONV_PAD + Lp, D), f32), ((Sp, POOL_PAD + Lp, D), f32)]
    scratch_shapes = ([pltpu.VMEM(s, d) for s, d in scratch]
                      + [pltpu.SemaphoreType.DMA((STAGE_SLOTS,))]
                      + [pltpu.VMEM(s, d) for s, d in buffers])

    blocks = [((Ss, Ls, D), 2), ((Ss, Ls, ps.shape[-1]), 1), ((Ss, CONV_W - 1, D), 2),
              ((POOL_HIST, Ss, D), 2), ((Sp, Lp, D), 2), ((Sp, Lp, pp.shape[-1]), 1),
              ((Sp, CONV_W - 1, D), 1), ((POOL_HIST, Bp, D), 1)]
    vmem_estimate = (sum(_vmem_bytes(s, d) for s, d in scratch + buffers)
                     + sum(2 * n * _vmem_bytes(s, f32) for s, n in blocks)
                     + sum(_vmem_bytes(a.shape[-2:], f32) for a in small)
                     + min(2, Sp * Lp // CHAIN_ROWS) * LIVE_RESULTS_PER_CHAIN
                     * _vmem_bytes((CHAIN_ROWS, D), f32))
    assert vmem_estimate <= V7X_VMEM_BYTES, vmem_estimate

    return pl.pallas_call(
        functools.partial(_fused_kernel, layer=layer, sample_steps=sample_steps,
                          prompt_tiles_per_seq=tiles_per_seq),
        grid=grid,
        in_specs=in_specs,
        out_specs=out_specs,
        out_shape=out_shape,
        scratch_shapes=scratch_shapes,
        compiler_params=pltpu.CompilerParams(
            dimension_semantics=("arbitrary",),
            vmem_limit_bytes=vmem_estimate),
    )(xs, ps, conv_hist, pool_hist, xp, pp, *small, *big)


def kernel(x_prompt, x_sample, p_prompt, p_sample, cache_conv, state_pool, g_pre, w_in,
           conv_w, conv_b, w_grp, pool_scale, w_a_out, w_b_out, w_o, g_post, w_ple, w_pgate):
    depth = w_in.shape[0]
    hp, hs = x_prompt, x_sample
    big = (w_in, w_grp, w_a_out, w_b_out, w_o, w_ple, w_pgate)
    pool_hist = jnp.transpose(state_pool, STATE_TIME_MAJOR)
    conv_p, conv_s, pool_p, pool_s = [], [], [], []
    for i in range(depth):
        small = (g_pre[i][None, :], conv_w, conv_b[i][None, :], pool_scale[i][None, :],
                 g_post[i][None, :])
        hs, nc_s, np_s, hp, nc_p, np_p = _layer(i, hs, p_sample[i], cache_conv, pool_hist,
                                                hp, p_prompt[i], small, big)
        conv_p.append(nc_p); pool_p.append(jnp.transpose(np_p, STATE_TIME_MAJOR))
        conv_s.append(nc_s); pool_s.append(jnp.transpose(np_s, STATE_TIME_MAJOR))

    def stacked(parts):
        return parts[0] if depth == 1 else jnp.concatenate(parts, axis=0)

    return (hp, hs, stacked(conv_p), stacked(conv_s), stacked(pool_p), stacked(pool_s))
```

```python
import functools
from typing import Any, NamedTuple

import jax
import jax.numpy as jnp
from jax import lax
from jax.experimental import pallas as pl
from jax.experimental.pallas import tpu as pltpu

D_MODEL = 1024
CONV_W = 3
POOL_WINDOWS = (2, 4, 8, 16)
N_POOL_GROUPS = len(POOL_WINDOWS)
POOL_GW = D_MODEL // N_POOL_GROUPS
POOL_HIST = max(POOL_WINDOWS) - 1
PAST_LEN = 4096
EPS = 1e-6

SEG_XA, SEG_BA, SEG_CA, SEG_ZA, SEG_XB, SEG_ZB, SEG_GA, SEG_GB = range(8)

CONV_PAD = 8
POOL_PAD = 16
assert CONV_PAD >= CONV_W - 1 and POOL_PAD >= POOL_HIST

SUBLANES, LANES = 8, 128
V7X_VMEM_BYTES = 64 * 1024 * 1024
LIVE_RESULTS_PER_CHAIN = 3

PROMPT_TILE = (1, 512)
SAMPLE_TILE = (8, 32)
CHAIN_ROWS = 256

STAGE_ROWS, STAGE_COLS, STAGE_SLOTS = 256, 1024, 6

MATS = ("w_in", "w_grp", "w_a", "w_b", "w_o", "w_ple", "w_pgate")

STATE_TIME_MAJOR = (0, 2, 1, 3)


class _Weights(NamedTuple):
    g_pre: Any
    conv_w: Any
    conv_b: Any
    pool_scale: Any
    g_post: Any
    mats: Any
    stream: Any = None

    def dot(self, lhs, name, *, group=None, col0=0, ncols=None):
        ref = self.mats[name] if group is None else self.mats[name].at[group]
        ncols = ref.shape[1] if ncols is None else ncols
        cols = slice(col0, col0 + ncols)
        if self.stream is None:
            return jnp.dot(lhs, ref[:, cols], preferred_element_type=jnp.float32)
        acc = None
        for r0 in range(0, ref.shape[0], STAGE_ROWS):
            self.stream.need(name, group, r0, col0 - col0 % STAGE_COLS)
            part = jnp.dot(lhs[:, r0:r0 + STAGE_ROWS], ref[r0:r0 + STAGE_ROWS, cols],
                           preferred_element_type=jnp.float32)
            acc = part if acc is None else acc + part
        return acc


class _Unit(NamedTuple):
    name: str
    group: Any
    row0: int
    col0: int
    rows: int
    cols: int


def _stream_units(mats):
    D = D_MODEL

    def chunks(name, col0=0, group=None):
        rows, cols = mats[name].shape[-2:]
        cols = min(cols, STAGE_COLS)
        return [_Unit(name, group, r0, col0, min(rows, STAGE_ROWS), cols)
                for r0 in range(0, rows, STAGE_ROWS)]

    def seg(s):
        return chunks("w_in", s * D)

    units = seg(SEG_XB) + seg(SEG_CA) + seg(SEG_XA) + seg(SEG_BA) + seg(SEG_ZA) + seg(SEG_ZB)
    for g in range(N_POOL_GROUPS):
        units += chunks("w_grp", group=g)
    units += chunks("w_a") + seg(SEG_GA) + chunks("w_b") + seg(SEG_GB)
    units += chunks("w_ple") + chunks("w_o") + chunks("w_pgate")
    return units


class _WeightStream:
    AHEAD = 4

    def __init__(self, layer, hbm, mats, stage, sem):
        self.layer, self.hbm, self.mats, self.stage, self.sem = layer, hbm, mats, stage, sem
        self.units = _stream_units(mats)
        self.index = {u[:4]: i for i, u in enumerate(self.units)}
        self.arrived = 0

    def _copy(self, i):
        u = self.units[i]
        lead = (self.layer,) if u.group is None else (self.layer, u.group)
        src = self.hbm[u.name].at[lead + (pl.ds(u.row0, u.rows), pl.ds(u.col0, u.cols))]
        slot = i % STAGE_SLOTS
        return pltpu.make_async_copy(src, self.stage.at[slot, pl.ds(0, u.rows), pl.ds(0, u.cols)],
                                     self.sem.at[slot])

    def start(self):
        for i in range(min(STAGE_SLOTS, len(self.units))):
            self._copy(i).start()

    def _arrive(self):
        i = self.arrived
        u = self.units[i]
        self._copy(i).wait()
        dst = self.mats[u.name] if u.group is None else self.mats[u.name].at[u.group]
        dst[u.row0:u.row0 + u.rows, u.col0:u.col0 + u.cols] = (
            self.stage[i % STAGE_SLOTS, 0:u.rows, 0:u.cols].astype(dst.dtype))
        if i + STAGE_SLOTS < len(self.units):
            self._copy(i + STAGE_SLOTS).start()
        self.arrived += 1

    def need(self, name, group, row0, col0):
        i = self.index[(name, group, row0, col0)]
        assert i <= self.arrived, "weights are read in another order than they are streamed"
        if self.arrived <= min(i + 1, len(self.units) - 1):
            while self.arrived < min(i + 1 + self.AHEAD, len(self.units)):
                self._arrive()

    def finish(self):
        while self.arrived < len(self.units):
            self._arrive()


def _rmsnorm(x, g):
    ms = jnp.mean(x * x, axis=-1, keepdims=True)
    return x * lax.rsqrt(ms + EPS) * g


def _sigmoid(x):
    return 0.5 * jnp.tanh(0.5 * x) + 0.5


def _silu(x):
    half = 0.5 * x
    return half * jnp.tanh(half) + half


def _shift_rows(v, k):
    return pltpu.roll(v, k, axis=v.ndim - 2)


def _trailing_sum(v, w):
    assert w & (w - 1) == 0
    k = 1
    while k < w:
        v = v + _shift_rows(v, k)
        k *= 2
    return v


_TAIL = "tail"
_STAY = "stay"


def _chain(s0, r0, Ss, R, pos0, x_ref, p_ref, y_ref, ubuf, xbuf, W, embed_first):
    D = x_ref.shape[-1]
    M = Ss * R
    f32, bf16 = jnp.float32, jnp.bfloat16
    streams = slice(s0, s0 + Ss)

    def embed():
        p = p_ref[streams, r0:r0 + R, :].reshape(M, p_ref.shape[-1])
        return W.dot(p.astype(bf16), "w_ple")

    if embed_first:
        pe = embed()
        yield _STAY
    x = x_ref[streams, r0:r0 + R, :].reshape(M, D)
    h = _rmsnorm(x, W.g_pre[...]).astype(bf16)
    yield

    def proj(seg, half=None):
        if half is None:
            return W.dot(h, "w_in", col0=seg * D, ncols=D)
        return W.dot(h, "w_in", col0=seg * D + half * (D // 2), ncols=D // 2)

    xb = proj(SEG_XB).reshape(Ss, R, D)
    yield
    xbuf[streams, POOL_PAD + r0:POOL_PAD + r0 + R, :] = xb
    u = []
    for k in range(2):
        ca = proj(SEG_CA, k)
        yield
        u.append(ca * proj(SEG_XA, k))
        yield
    u = jnp.concatenate(u, axis=-1).reshape(Ss, R, D)
    ubuf[streams, CONV_PAD + r0:CONV_PAD + r0 + R, :] = u

    pos = lax.broadcasted_iota(jnp.int32, (Ss, R, POOL_GW), 1) + (pos0 + r0)
    d = []
    for g, w in enumerate(POOL_WINDOWS):
        cols = slice(g * POOL_GW, (g + 1) * POOL_GW)
        ext = xbuf[streams, r0:r0 + POOL_PAD + R, cols]
        xg = ext[:, POOL_PAD:, :]
        s = _trailing_sum(ext, w)[:, POOL_PAD:, :]
        cnt = jnp.minimum(pos + 1, w).astype(f32)
        d.append((s / cnt - xg).reshape(M, POOL_GW).astype(bf16))

    ext = ubuf[streams, r0:r0 + CONV_PAD + R, :]
    conv = W.conv_b[...] + W.conv_w[CONV_W - 1:CONV_W, :] * ext[:, CONV_PAD:, :]
    for k in range(CONV_W - 1):
        shifted = _shift_rows(ext, CONV_W - 1 - k)[:, CONV_PAD:, :]
        conv = conv + W.conv_w[k:k + 1, :] * shifted
    conv = conv.reshape(M, D)
    a = []
    for k in range(2):
        ba = proj(SEG_BA, k)
        yield
        a.append(((ba * conv[:, k * (D // 2):(k + 1) * (D // 2)]) * _silu(proj(SEG_ZA, k))).astype(bf16))
        yield
    a = jnp.concatenate(a, axis=-1)
    zb = proj(SEG_ZB)
    yield
    mixed = []
    for g in range(N_POOL_GROUPS):
        mixed.append(W.dot(d[g], "w_grp", group=g))
        yield
    mixed = jnp.concatenate(mixed, axis=-1) * W.pool_scale[...]
    b = (mixed * _silu(zb)).astype(bf16)
    ya = W.dot(a, "w_a")
    yield
    m = _sigmoid(proj(SEG_GA)) * ya
    yield
    yb = W.dot(b, "w_b")
    yield _TAIL
    m = (m + _sigmoid(proj(SEG_GB)) * yb).astype(bf16)
    yield
    if not embed_first:
        pe = embed()
        yield

    n_parts = 1 if embed_first else 2
    parts = [(k * M // n_parts, (k + 1) * M // n_parts) for k in range(n_parts)]
    o = []
    for lo, hi in parts:
        o.append(W.dot(m[lo:hi], "w_o"))
        yield
    for (lo, hi), o_part in zip(parts, o):
        x1 = x[lo:hi] + _rmsnorm(o_part, W.g_post[...])
        gate = W.dot(x1.astype(bf16), "w_pgate")
        yield
        y = x1 + pe[lo:hi] * _sigmoid(gate)
        if Ss > 1:
            y_ref[s0 + lo // R:s0 + hi // R, r0:r0 + R, :] = y.reshape((hi - lo) // R, R, D)
        else:
            y_ref[streams, r0 + lo:r0 + hi, :] = y.reshape(1, hi - lo, D)


def _run_interleaved(chains):
    done = object()
    prev = None
    for cur in chains:
        for marker in cur:
            if marker is _STAY:
                continue
            if prev is not None and next(prev, done) is done:
                prev = None
            if marker is _TAIL:
                break
        if prev is not None:
            for _ in prev:
                pass
        prev = cur
    for _ in prev:
        pass


def _chain_shape(S, L):
    if L >= CHAIN_ROWS:
        assert L % CHAIN_ROWS == 0
        return 1, CHAIN_ROWS
    assert CHAIN_ROWS % L == 0
    return min(S, CHAIN_ROWS // L), L


def _tile(x_ref, p_ref, y_ref, ubuf, xbuf, W, *, t, pos0, load_history):
    S, L, _ = x_ref.shape
    Ss, R = _chain_shape(S, L)

    @pl.when(t == 0)
    def _():
        ubuf[:, 0:CONV_PAD, :] = jnp.zeros((S, CONV_PAD, ubuf.shape[-1]), ubuf.dtype)
        xbuf[:, 0:POOL_PAD, :] = jnp.zeros((S, POOL_PAD, xbuf.shape[-1]), xbuf.dtype)
        if load_history is not None:
            load_history()

    starts = [(s0, r0) for s0 in range(0, S, Ss) for r0 in range(0, L, R)]
    assert W.stream is None or len(starts) == 1
    _run_interleaved(_chain(s0, r0, Ss, R, pos0, x_ref, p_ref, y_ref, ubuf, xbuf, W,
                            embed_first=i < len(starts) - 1)
                     for i, (s0, r0) in enumerate(starts))

    ubuf[:, CONV_PAD - (CONV_W - 1):CONV_PAD, :] = ubuf[:, CONV_PAD + L - (CONV_W - 1):CONV_PAD + L, :]
    xbuf[:, POOL_PAD - POOL_HIST:POOL_PAD, :] = xbuf[:, POOL_PAD + L - POOL_HIST:POOL_PAD + L, :]


def _fused_kernel(xs_ref, ps_ref, ch_ref, ph_ref, xp_ref, pp_ref,
                  g_pre_ref, conv_w_ref, conv_b_ref, pool_scale_ref, g_post_ref,
                  w_in_hbm, w_grp_hbm, w_a_hbm, w_b_hbm, w_o_hbm, w_ple_hbm, w_pgate_hbm,
                  ys_ref, ncs_ref, nps_ref, yp_ref, ncp_ref, npp_ref,
                  w_in, w_grp, w_a, w_b, w_o, w_ple, w_pgate, stage, sem,
                  ubuf_s, xbuf_s, ubuf_p, xbuf_p, *, layer, sample_steps, prompt_tiles_per_seq):
    j = pl.program_id(0)
    hbm = dict(zip(MATS, (w_in_hbm, w_grp_hbm, w_a_hbm, w_b_hbm, w_o_hbm, w_ple_hbm, w_pgate_hbm)))
    mats = dict(zip(MATS, (w_in, w_grp, w_a, w_b, w_o, w_ple, w_pgate)))
    W = _Weights(g_pre_ref, conv_w_ref, conv_b_ref, pool_scale_ref, g_post_ref, mats)
    Ls, Lp = xs_ref.shape[1], xp_ref.shape[1]

    def sample_history():
        ubuf_s[:, CONV_PAD - (CONV_W - 1):CONV_PAD, :] = ch_ref[...]
        for k in range(POOL_HIST):
            xbuf_s[:, POOL_PAD - POOL_HIST + k, :] = ph_ref[k]

    def sample_tile(weights):
        _tile(xs_ref, ps_ref, ys_ref, ubuf_s, xbuf_s, weights, t=0, pos0=PAST_LEN,
              load_history=sample_history)
        ncs_ref[...] = ubuf_s[:, CONV_PAD + Ls - (CONV_W - 1):CONV_PAD + Ls, :]
        for k in range(POOL_HIST):
            nps_ref[k] = xbuf_s[:, POOL_PAD + Ls - POOL_HIST + k, :]

    @pl.when(j == 0)
    def _():
        stream = _WeightStream(layer, hbm, mats, stage, sem)
        stream.start()
        sample_tile(W._replace(stream=stream))
        stream.finish()

    @pl.when((j > 0) & (j < sample_steps))
    def _():
        sample_tile(W)

    @pl.when(j == sample_steps)
    def _():
        npp_ref[...] = jnp.zeros(npp_ref.shape, npp_ref.dtype)

    @pl.when(j >= sample_steps)
    def _():
        q = j - sample_steps
        b, t = q // prompt_tiles_per_seq, lax.rem(q, prompt_tiles_per_seq)
        _tile(xp_ref, pp_ref, yp_ref, ubuf_p, xbuf_p, W, t=t, pos0=t * Lp, load_history=None)
        ncp_ref[...] = ubuf_p[:, CONV_PAD + Lp - (CONV_W - 1):CONV_PAD + Lp, :]

        @pl.when(t == prompt_tiles_per_seq - 1)
        def _():
            row_shape = npp_ref.shape[1:]
            mine = lax.broadcasted_iota(jnp.int32, row_shape, 0) == b
            for k in range(POOL_HIST):
                row = xbuf_p[0, POOL_PAD + Lp - POOL_HIST + k:POOL_PAD + Lp - POOL_HIST + k + 1, :]
                npp_ref[k] = jnp.where(mine, jnp.broadcast_to(row, row_shape), npp_ref[k])


def _vmem_bytes(shape, dtype):
    itemsize = jnp.dtype(dtype).itemsize
    sublanes = SUBLANES * 4 // itemsize
    *lead, rows, cols = shape
    n = itemsize * (-(-rows // sublanes) * sublanes) * (-(-cols // LANES) * LANES)
    for s in lead:
        n *= s
    return n


def _layer(layer, xs, ps, conv_hist, pool_hist, xp, pp, small, big):
    f32, bf16 = jnp.float32, jnp.bfloat16
    Bs, Ts, D = xs.shape
    Bp, Tp, _ = xp.shape
    Ss, Ls = SAMPLE_TILE
    Sp, Lp = PROMPT_TILE
    assert Ls == Ts and Bs % Ss == 0 and Sp == 1 and Tp % Lp == 0 and min(Ls, Lp) >= POOL_HIST
    sample_steps = Bs // Ss
    tiles_per_seq = Tp // Lp
    grid = (sample_steps + Bp * tiles_per_seq,)

    def sample_block(j):
        return jnp.minimum(j, sample_steps - 1)

    def prompt_block(j):
        q = jnp.maximum(j - sample_steps, 0)
        return q // tiles_per_seq, q % tiles_per_seq

    def sample_rows(width):
        return pl.BlockSpec((Ss, Ls, width), lambda j: (sample_block(j), 0, 0))

    def prompt_rows(width):
        return pl.BlockSpec((Sp, Lp, width), lambda j: prompt_block(j) + (0,))

    def sample_conv(depth_index):
        return pl.BlockSpec((None, Ss, CONV_W - 1, D),
                            lambda j: (depth_index, sample_block(j), 0, 0))

    def sample_pool(depth_index):
        return pl.BlockSpec((None, POOL_HIST, Ss, D),
                            lambda j: (depth_index, 0, sample_block(j), 0))

    prompt_conv = pl.BlockSpec((None, Sp, CONV_W - 1, D), lambda j: (0, prompt_block(j)[0], 0, 0))
    prompt_pool = pl.BlockSpec((None, POOL_HIST, Bp, D), lambda j: (0, 0, 0, 0))

    def resident(arr):
        return pl.BlockSpec(arr.shape, lambda j: (0,) * arr.ndim, pipeline_mode=pl.Buffered(1))

    g_pre, conv_w, conv_b, pool_scale, g_post = small
    small_specs = [resident(g_pre),
                   pl.BlockSpec((None,) + conv_w.shape[1:], lambda j: (layer, 0, 0),
                                pipeline_mode=pl.Buffered(1)),
                   resident(conv_b), resident(pool_scale), resident(g_post)]
    in_specs = ([sample_rows(D), sample_rows(ps.shape[-1]), sample_conv(layer), sample_pool(layer),
                 prompt_rows(D), prompt_rows(pp.shape[-1])]
                + small_specs + [pl.BlockSpec(memory_space=pl.ANY)] * len(big))
    out_specs = [sample_rows(D), sample_conv(0), sample_pool(0),
                 prompt_rows(D), prompt_conv, prompt_pool]
    out_shape = [jax.ShapeDtypeStruct((Bs, Ts, D), f32),
                 jax.ShapeDtypeStruct((1, Bs, CONV_W - 1, D), f32),
                 jax.ShapeDtypeStruct((1, POOL_HIST, Bs, D), f32),
                 jax.ShapeDtypeStruct((Bp, Tp, D), f32),
                 jax.ShapeDtypeStruct((1, Bp, CONV_W - 1, D), f32),
                 jax.ShapeDtypeStruct((1, POOL_HIST, Bp, D), f32)]
    scratch = [(w.shape[1:], bf16) for w in big]
    scratch += [((STAGE_SLOTS, STAGE_ROWS, STAGE_COLS), f32)]
    buffers = [((Ss, CONV_PAD + Ls, D), f32), ((Ss, POOL_PAD + Ls, D), f32),
               ((Sp, CONV_PAD + Lp, D), f32), ((Sp, POOL_PAD + Lp, D), f32)]
    scratch_shapes = ([pltpu.VMEM(s, d) for s, d in scratch]
                      + [pltpu.SemaphoreType.DMA((STAGE_SLOTS,))]
                      + [pltpu.VMEM(s, d) for s, d in buffers])

    blocks = [((Ss, Ls, D), 2), ((Ss, Ls, ps.shape[-1]), 1), ((Ss, CONV_W - 1, D), 2),
              ((POOL_HIST, Ss, D), 2), ((Sp, Lp, D), 2), ((Sp, Lp, pp.shape[-1]), 1),
              ((Sp, CONV_W - 1, D), 1), ((POOL_HIST, Bp, D), 1)]
    vmem_estimate = (sum(_vmem_bytes(s, d) for s, d in scratch + buffers)
                     + sum(2 * n * _vmem_bytes(s, f32) for s, n in blocks)
                     + sum(_vmem_bytes(a.shape[-2:], f32) for a in small)
                     + min(2, Lp // CHAIN_ROWS) * LIVE_RESULTS_PER_CHAIN
                     * _vmem_bytes((CHAIN_ROWS, D), f32))
    assert vmem_estimate <= V7X_VMEM_BYTES, vmem_estimate

    return pl.pallas_call(
        functools.partial(_fused_kernel, layer=layer, sample_steps=sample_steps,
                          prompt_tiles_per_seq=tiles_per_seq),
        grid=grid,
        in_specs=in_specs,
        out_specs=out_specs,
        out_shape=out_shape,
        scratch_shapes=scratch_shapes,
        compiler_params=pltpu.CompilerParams(
            dimension_semantics=("arbitrary",),
            vmem_limit_bytes=vmem_estimate),
    )(xs, ps, conv_hist, pool_hist, xp, pp, *small, *big)


def kernel(x_prompt, x_sample, p_prompt, p_sample, cache_conv, state_pool, g_pre, w_in,
           conv_w, conv_b, w_grp, pool_scale, w_a_out, w_b_out, w_o, g_post, w_ple, w_pgate):
    depth = w_in.shape[0]
    hp, hs = x_prompt, x_sample
    big = (w_in, w_grp, w_a_out, w_b_out, w_o, w_ple, w_pgate)
    pool_hist = jnp.transpose(state_pool, STATE_TIME_MAJOR)
    conv_p, conv_s, pool_p, pool_s = [], [], [], []
    for i in range(depth):
        small = (g_pre[i][None, :], conv_w, conv_b[i][None, :], pool_scale[i][None, :],
                 g_post[i][None, :])
        hs, nc_s, np_s, hp, nc_p, np_p = _layer(i, hs, p_sample[i], cache_conv, pool_hist,
                                                hp, p_prompt[i], small, big)
        conv_p.append(nc_p); pool_p.append(jnp.transpose(np_p, STATE_TIME_MAJOR))
        conv_s.append(nc_s); pool_s.append(jnp.transpose(np_s, STATE_TIME_MAJOR))

    def stacked(parts):
        return parts[0] if depth == 1 else jnp.concatenate(parts, axis=0)

    return (hp, hs, stacked(conv_p), stacked(conv_s), stacked(pool_p), stacked(pool_s))
```

```python
import functools
from typing import Any, NamedTuple

import jax
import jax.numpy as jnp
from jax import lax
from jax.experimental import pallas as pl
from jax.experimental.pallas import tpu as pltpu

D_MODEL = 1024
CONV_W = 3
POOL_WINDOWS = (2, 4, 8, 16)
N_POOL_GROUPS = len(POOL_WINDOWS)
POOL_GW = D_MODEL // N_POOL_GROUPS
POOL_HIST = max(POOL_WINDOWS) - 1
PAST_LEN = 4096
EPS = 1e-6

SEG_XA, SEG_BA, SEG_CA, SEG_ZA, SEG_XB, SEG_ZB, SEG_GA, SEG_GB = range(8)

CONV_PAD = 8
POOL_PAD = 16
assert CONV_PAD >= CONV_W - 1 and POOL_PAD >= POOL_HIST

SUBLANES, LANES = 8, 128
V7X_VMEM_BYTES = 64 * 1024 * 1024
LIVE_RESULTS_PER_CHAIN = 3

PROMPT_TILE = (1, 512)
SAMPLE_TILE = (8, 32)
CHAIN_ROWS = 256

STAGE_ROWS, STAGE_COLS, STAGE_SLOTS = 256, 1024, 6

MATS = ("w_in", "w_grp", "w_a", "w_b", "w_o", "w_ple", "w_pgate")

STATE_TIME_MAJOR = (0, 2, 1, 3)


class _Weights(NamedTuple):
    g_pre: Any
    conv_w: Any
    conv_b: Any
    pool_scale: Any
    g_post: Any
    mats: Any
    stream: Any = None

    def dot(self, lhs, name, *, group=None, col0=0, ncols=None):
        ref = self.mats[name] if group is None else self.mats[name].at[group]
        ncols = ref.shape[1] if ncols is None else ncols
        cols = slice(col0, col0 + ncols)
        if self.stream is None:
            return jnp.dot(lhs, ref[:, cols], preferred_element_type=jnp.float32)
        out = []
        for c0 in range(col0, col0 + ncols, min(ncols, STAGE_COLS)):
            acc = None
            for r0 in range(0, ref.shape[0], STAGE_ROWS):
                self.stream.need(name, group, r0, c0)
                part = jnp.dot(lhs[:, r0:r0 + STAGE_ROWS],
                               ref[r0:r0 + STAGE_ROWS, c0:c0 + min(ncols, STAGE_COLS)],
                               preferred_element_type=jnp.float32)
                acc = part if acc is None else acc + part
            out.append(acc)
        return out[0] if len(out) == 1 else jnp.concatenate(out, axis=-1)


class _Unit(NamedTuple):
    name: str
    group: Any
    row0: int
    col0: int
    rows: int
    cols: int


def _stream_units(mats):
    D = D_MODEL

    def chunks(name, col0=0, group=None):
        rows, cols = mats[name].shape[-2:]
        cols = min(cols, STAGE_COLS)
        return [_Unit(name, group, r0, col0, min(rows, STAGE_ROWS), cols)
                for r0 in range(0, rows, STAGE_ROWS)]

    def seg(s):
        return chunks("w_in", s * D)

    units = seg(SEG_XB) + seg(SEG_CA) + seg(SEG_XA) + seg(SEG_BA) + seg(SEG_ZA) + seg(SEG_ZB)
    for g in range(N_POOL_GROUPS):
        units += chunks("w_grp", group=g)
    units += chunks("w_a") + seg(SEG_GA) + seg(SEG_GB) + chunks("w_b")
    units += chunks("w_ple") + chunks("w_o") + chunks("w_pgate")
    return units


class _WeightStream:
    AHEAD = 4

    def __init__(self, layer, hbm, mats, stage, sem):
        self.layer, self.hbm, self.mats, self.stage, self.sem = layer, hbm, mats, stage, sem
        self.units = _stream_units(mats)
        self.index = {u[:4]: i for i, u in enumerate(self.units)}
        self.arrived = 0

    def _copy(self, i):
        u = self.units[i]
        lead = (self.layer,) if u.group is None else (self.layer, u.group)
        src = self.hbm[u.name].at[lead + (pl.ds(u.row0, u.rows), pl.ds(u.col0, u.cols))]
        slot = i % STAGE_SLOTS
        return pltpu.make_async_copy(src, self.stage.at[slot, pl.ds(0, u.rows), pl.ds(0, u.cols)],
                                     self.sem.at[slot])

    def start(self):
        for i in range(min(STAGE_SLOTS, len(self.units))):
            self._copy(i).start()

    def _arrive(self):
        i = self.arrived
        u = self.units[i]
        self._copy(i).wait()
        dst = self.mats[u.name] if u.group is None else self.mats[u.name].at[u.group]
        dst[u.row0:u.row0 + u.rows, u.col0:u.col0 + u.cols] = (
            self.stage[i % STAGE_SLOTS, 0:u.rows, 0:u.cols].astype(dst.dtype))
        if i + STAGE_SLOTS < len(self.units):
            self._copy(i + STAGE_SLOTS).start()
        self.arrived += 1

    def need(self, name, group, row0, col0):
        i = self.index[(name, group, row0, col0)]
        assert i <= self.arrived, "weights are read in another order than they are streamed"
        if self.arrived <= min(i + 1, len(self.units) - 1):
            while self.arrived < min(i + 1 + self.AHEAD, len(self.units)):
                self._arrive()

    def finish(self):
        while self.arrived < len(self.units):
            self._arrive()


def _rmsnorm(x, g):
    ms = jnp.mean(x * x, axis=-1, keepdims=True)
    return x * lax.rsqrt(ms + EPS) * g


def _sigmoid(x):
    return 0.5 * jnp.tanh(0.5 * x) + 0.5


def _silu(x):
    half = 0.5 * x
    return half * jnp.tanh(half) + half


def _shift_rows(v, k):
    return pltpu.roll(v, k, axis=v.ndim - 2)


def _trailing_sum(v, w):
    assert w & (w - 1) == 0
    k = 1
    while k < w:
        v = v + _shift_rows(v, k)
        k *= 2
    return v


_TAIL = "tail"
_STAY = "stay"


def _chain(s0, r0, Ss, R, pos0, x_ref, p_ref, y_ref, ubuf, xbuf, W, embed_first):
    D = x_ref.shape[-1]
    M = Ss * R
    f32, bf16 = jnp.float32, jnp.bfloat16
    streams = slice(s0, s0 + Ss)

    def embed():
        p = p_ref[streams, r0:r0 + R, :].reshape(M, p_ref.shape[-1])
        return W.dot(p.astype(bf16), "w_ple")

    if embed_first:
        pe = embed()
        yield _STAY
    x = x_ref[streams, r0:r0 + R, :].reshape(M, D)
    h = _rmsnorm(x, W.g_pre[...]).astype(bf16)
    yield

    def proj(seg):
        return W.dot(h, "w_in", col0=seg * D, ncols=D)

    xb = proj(SEG_XB).reshape(Ss, R, D)
    yield
    xbuf[streams, POOL_PAD + r0:POOL_PAD + r0 + R, :] = xb
    ca = proj(SEG_CA)
    yield
    u = (ca * proj(SEG_XA)).reshape(Ss, R, D)
    yield
    ubuf[streams, CONV_PAD + r0:CONV_PAD + r0 + R, :] = u

    pos = lax.broadcasted_iota(jnp.int32, (Ss, R, POOL_GW), 1) + (pos0 + r0)
    d = []
    for g, w in enumerate(POOL_WINDOWS):
        cols = slice(g * POOL_GW, (g + 1) * POOL_GW)
        ext = xbuf[streams, r0:r0 + POOL_PAD + R, cols]
        xg = ext[:, POOL_PAD:, :]
        s = _trailing_sum(ext, w)[:, POOL_PAD:, :]
        cnt = jnp.minimum(pos + 1, w).astype(f32)
        d.append((s / cnt - xg).reshape(M, POOL_GW).astype(bf16))

    ext = ubuf[streams, r0:r0 + CONV_PAD + R, :]
    conv = W.conv_b[...] + W.conv_w[CONV_W - 1:CONV_W, :] * ext[:, CONV_PAD:, :]
    for k in range(CONV_W - 1):
        shifted = _shift_rows(ext, CONV_W - 1 - k)[:, CONV_PAD:, :]
        conv = conv + W.conv_w[k:k + 1, :] * shifted
    ba = proj(SEG_BA)
    yield
    a = ((ba * conv.reshape(M, D)) * _silu(proj(SEG_ZA))).astype(bf16)
    yield
    zb = proj(SEG_ZB)
    yield
    mixed = []
    for g in range(N_POOL_GROUPS):
        mixed.append(W.dot(d[g], "w_grp", group=g))
        yield
    mixed = jnp.concatenate(mixed, axis=-1) * W.pool_scale[...]
    b = (mixed * _silu(zb)).astype(bf16)
    ya = W.dot(a, "w_a")
    yield
    assert SEG_GB == SEG_GA + 1
    gates = W.dot(h, "w_in", col0=SEG_GA * D, ncols=2 * D)
    m = _sigmoid(gates[:, :D]) * ya
    yield
    yb = W.dot(b, "w_b")
    yield _TAIL
    m = (m + _sigmoid(gates[:, D:]) * yb).astype(bf16)
    yield
    if not embed_first:
        pe = embed()
        yield

    n_parts = 1 if embed_first else 2
    parts = [(k * M // n_parts, (k + 1) * M // n_parts) for k in range(n_parts)]
    o = []
    for lo, hi in parts:
        o.append(W.dot(m[lo:hi], "w_o"))
        yield
    for (lo, hi), o_part in zip(parts, o):
        x1 = x[lo:hi] + _rmsnorm(o_part, W.g_post[...])
        gate = W.dot(x1.astype(bf16), "w_pgate")
        yield
        y = x1 + pe[lo:hi] * _sigmoid(gate)
        if Ss > 1:
            y_ref[s0 + lo // R:s0 + hi // R, r0:r0 + R, :] = y.reshape((hi - lo) // R, R, D)
        else:
            y_ref[streams, r0 + lo:r0 + hi, :] = y.reshape(1, hi - lo, D)


def _run_interleaved(chains):
    done = object()
    prev = None
    for cur in chains:
        for marker in cur:
            if marker is _STAY:
                continue
            if prev is not None and next(prev, done) is done:
                prev = None
            if marker is _TAIL:
                break
        if prev is not None:
            for _ in prev:
                pass
        prev = cur
    for _ in prev:
        pass


def _chain_shape(S, L):
    if L >= CHAIN_ROWS:
        assert L % CHAIN_ROWS == 0
        return 1, CHAIN_ROWS
    assert CHAIN_ROWS % L == 0
    return min(S, CHAIN_ROWS // L), L


def _tile(x_ref, p_ref, y_ref, ubuf, xbuf, W, *, t, pos0, load_history):
    S, L, _ = x_ref.shape
    Ss, R = _chain_shape(S, L)

    @pl.when(t == 0)
    def _():
        ubuf[:, 0:CONV_PAD, :] = jnp.zeros((S, CONV_PAD, ubuf.shape[-1]), ubuf.dtype)
        xbuf[:, 0:POOL_PAD, :] = jnp.zeros((S, POOL_PAD, xbuf.shape[-1]), xbuf.dtype)
        if load_history is not None:
            load_history()

    starts = [(s0, r0) for s0 in range(0, S, Ss) for r0 in range(0, L, R)]
    assert W.stream is None or len(starts) == 1
    _run_interleaved(_chain(s0, r0, Ss, R, pos0, x_ref, p_ref, y_ref, ubuf, xbuf, W,
                            embed_first=i < len(starts) - 1)
                     for i, (s0, r0) in enumerate(starts))

    ubuf[:, CONV_PAD - (CONV_W - 1):CONV_PAD, :] = ubuf[:, CONV_PAD + L - (CONV_W - 1):CONV_PAD + L, :]
    xbuf[:, POOL_PAD - POOL_HIST:POOL_PAD, :] = xbuf[:, POOL_PAD + L - POOL_HIST:POOL_PAD + L, :]


def _fused_kernel(xs_ref, ps_ref, ch_ref, ph_ref, xp_ref, pp_ref,
                  g_pre_ref, conv_w_ref, conv_b_ref, pool_scale_ref, g_post_ref,
                  w_in_hbm, w_grp_hbm, w_a_hbm, w_b_hbm, w_o_hbm, w_ple_hbm, w_pgate_hbm,
                  ys_ref, ncs_ref, nps_ref, yp_ref, ncp_ref, npp_ref,
                  w_in, w_grp, w_a, w_b, w_o, w_ple, w_pgate, stage, sem,
                  ubuf_s, xbuf_s, ubuf_p, xbuf_p, *, layer, sample_steps, prompt_tiles_per_seq):
    j = pl.program_id(0)
    hbm = dict(zip(MATS, (w_in_hbm, w_grp_hbm, w_a_hbm, w_b_hbm, w_o_hbm, w_ple_hbm, w_pgate_hbm)))
    mats = dict(zip(MATS, (w_in, w_grp, w_a, w_b, w_o, w_ple, w_pgate)))
    W = _Weights(g_pre_ref, conv_w_ref, conv_b_ref, pool_scale_ref, g_post_ref, mats)
    Ls, Lp = xs_ref.shape[1], xp_ref.shape[1]

    def sample_history():
        ubuf_s[:, CONV_PAD - (CONV_W - 1):CONV_PAD, :] = ch_ref[...]
        for k in range(POOL_HIST):
            xbuf_s[:, POOL_PAD - POOL_HIST + k, :] = ph_ref[k]

    def sample_tile(weights):
        _tile(xs_ref, ps_ref, ys_ref, ubuf_s, xbuf_s, weights, t=0, pos0=PAST_LEN,
              load_history=sample_history)
        ncs_ref[...] = ubuf_s[:, CONV_PAD + Ls - (CONV_W - 1):CONV_PAD + Ls, :]
        for k in range(POOL_HIST):
            nps_ref[k] = xbuf_s[:, POOL_PAD + Ls - POOL_HIST + k, :]

    @pl.when(j == 0)
    def _():
        stream = _WeightStream(layer, hbm, mats, stage, sem)
        stream.start()
        sample_tile(W._replace(stream=stream))
        stream.finish()

    @pl.when((j > 0) & (j < sample_steps))
    def _():
        sample_tile(W)

    @pl.when(j == sample_steps)
    def _():
        npp_ref[...] = jnp.zeros(npp_ref.shape, npp_ref.dtype)

    @pl.when(j >= sample_steps)
    def _():
        q = j - sample_steps
        b, t = q // prompt_tiles_per_seq, lax.rem(q, prompt_tiles_per_seq)
        _tile(xp_ref, pp_ref, yp_ref, ubuf_p, xbuf_p, W, t=t, pos0=t * Lp, load_history=None)
        ncp_ref[...] = ubuf_p[:, CONV_PAD + Lp - (CONV_W - 1):CONV_PAD + Lp, :]

        @pl.when(t == prompt_tiles_per_seq - 1)
        def _():
            row_shape = npp_ref.shape[1:]
            mine = lax.broadcasted_iota(jnp.int32, row_shape, 0) == b
            for k in range(POOL_HIST):
                row = xbuf_p[0, POOL_PAD + Lp - POOL_HIST + k:POOL_PAD + Lp - POOL_HIST + k + 1, :]
                npp_ref[k] = jnp.where(mine, jnp.broadcast_to(row, row_shape), npp_ref[k])


def _vmem_bytes(shape, dtype):
    itemsize = jnp.dtype(dtype).itemsize
    sublanes = SUBLANES * 4 // itemsize
    *lead, rows, cols = shape
    n = itemsize * (-(-rows // sublanes) * sublanes) * (-(-cols // LANES) * LANES)
    for s in lead:
        n *= s
    return n


def _layer(layer, xs, ps, conv_hist, pool_hist, xp, pp, small, big):
    f32, bf16 = jnp.float32, jnp.bfloat16
    Bs, Ts, D = xs.shape
    Bp, Tp, _ = xp.shape
    Ss, Ls = SAMPLE_TILE
    Sp, Lp = PROMPT_TILE
    assert Ls == Ts and Bs % Ss == 0 and Sp == 1 and Tp % Lp == 0 and min(Ls, Lp) >= POOL_HIST
    sample_steps = Bs // Ss
    tiles_per_seq = Tp // Lp
    grid = (sample_steps + Bp * tiles_per_seq,)

    def sample_block(j):
        return jnp.minimum(j, sample_steps - 1)

    def prompt_block(j):
        q = jnp.maximum(j - sample_steps, 0)
        return q // tiles_per_seq, q % tiles_per_seq

    def sample_rows(width):
        return pl.BlockSpec((Ss, Ls, width), lambda j: (sample_block(j), 0, 0))

    def prompt_rows(width):
        return pl.BlockSpec((Sp, Lp, width), lambda j: prompt_block(j) + (0,))

    def sample_conv(depth_index):
        return pl.BlockSpec((None, Ss, CONV_W - 1, D),
                            lambda j: (depth_index, sample_block(j), 0, 0))

    def sample_pool(depth_index):
        return pl.BlockSpec((None, POOL_HIST, Ss, D),
                            lambda j: (depth_index, 0, sample_block(j), 0))

    prompt_conv = pl.BlockSpec((None, Sp, CONV_W - 1, D), lambda j: (0, prompt_block(j)[0], 0, 0))
    prompt_pool = pl.BlockSpec((None, POOL_HIST, Bp, D), lambda j: (0, 0, 0, 0))

    def resident(arr):
        return pl.BlockSpec(arr.shape, lambda j: (0,) * arr.ndim, pipeline_mode=pl.Buffered(1))

    g_pre, conv_w, conv_b, pool_scale, g_post = small
    small_specs = [resident(g_pre),
                   pl.BlockSpec((None,) + conv_w.shape[1:], lambda j: (layer, 0, 0),
                                pipeline_mode=pl.Buffered(1)),
                   resident(conv_b), resident(pool_scale), resident(g_post)]
    in_specs = ([sample_rows(D), sample_rows(ps.shape[-1]), sample_conv(layer), sample_pool(layer),
                 prompt_rows(D), prompt_rows(pp.shape[-1])]
                + small_specs + [pl.BlockSpec(memory_space=pl.ANY)] * len(big))
    out_specs = [sample_rows(D), sample_conv(0), sample_pool(0),
                 prompt_rows(D), prompt_conv, prompt_pool]
    out_shape = [jax.ShapeDtypeStruct((Bs, Ts, D), f32),
                 jax.ShapeDtypeStruct((1, Bs, CONV_W - 1, D), f32),
                 jax.ShapeDtypeStruct((1, POOL_HIST, Bs, D), f32),
                 jax.ShapeDtypeStruct((Bp, Tp, D), f32),
                 jax.ShapeDtypeStruct((1, Bp, CONV_W - 1, D), f32),
                 jax.ShapeDtypeStruct((1, POOL_HIST, Bp, D), f32)]
    scratch = [(w.shape[1:], bf16) for w in big]
    scratch += [((STAGE_SLOTS, STAGE_ROWS, STAGE_COLS), f32)]
    buffers = [((Ss, CONV_PAD + Ls, D), f32), ((Ss, POOL_PAD + Ls, D), f32),
               ((Sp, CONV_PAD + Lp, D), f32), ((Sp, POOL_PAD + Lp, D), f32)]
    scratch_shapes = ([pltpu.VMEM(s, d) for s, d in scratch]
                      + [pltpu.SemaphoreType.DMA((STAGE_SLOTS,))]
                      + [pltpu.VMEM(s, d) for s, d in buffers])

    blocks = [((Ss, Ls, D), 2), ((Ss, Ls, ps.shape[-1]), 1), ((Ss, CONV_W - 1, D), 2),
              ((POOL_HIST, Ss, D), 2), ((Sp, Lp, D), 2), ((Sp, Lp, pp.shape[-1]), 1),
              ((Sp, CONV_W - 1, D), 1), ((POOL_HIST, Bp, D), 1)]
    vmem_estimate = (sum(_vmem_bytes(s, d) for s, d in scratch + buffers)
                     + sum(2 * n * _vmem_bytes(s, f32) for s, n in blocks)
                     + sum(_vmem_bytes(a.shape[-2:], f32) for a in small)
                     + min(2, Lp // CHAIN_ROWS) * LIVE_RESULTS_PER_CHAIN
                     * _vmem_bytes((CHAIN_ROWS, D), f32))
    assert vmem_estimate <= V7X_VMEM_BYTES, vmem_estimate

    return pl.pallas_call(
        functools.partial(_fused_kernel, layer=layer, sample_steps=sample_steps,
                          prompt_tiles_per_seq=tiles_per_seq),
        grid=grid,
        in_specs=in_specs,
        out_specs=out_specs,
        out_shape=out_shape,
        scratch_shapes=scratch_shapes,
        compiler_params=pltpu.CompilerParams(
            dimension_semantics=("arbitrary",),
            vmem_limit_bytes=vmem_estimate),
    )(xs, ps, conv_hist, pool_hist, xp, pp, *small, *big)


def kernel(x_prompt, x_sample, p_prompt, p_sample, cache_conv, state_pool, g_pre, w_in,
           conv_w, conv_b, w_grp, pool_scale, w_a_out, w_b_out, w_o, g_post, w_ple, w_pgate):
    depth = w_in.shape[0]
    hp, hs = x_prompt, x_sample
    big = (w_in, w_grp, w_a_out, w_b_out, w_o, w_ple, w_pgate)
    pool_hist = jnp.transpose(state_pool, STATE_TIME_MAJOR)
    conv_p, conv_s, pool_p, pool_s = [], [], [], []
    for i in range(depth):
        small = (g_pre[i][None, :], conv_w, conv_b[i][None, :], pool_scale[i][None, :],
                 g_post[i][None, :])
        hs, nc_s, np_s, hp, nc_p, np_p = _layer(i, hs, p_sample[i], cache_conv, pool_hist,
                                                hp, p_prompt[i], small, big)
        conv_p.append(nc_p); pool_p.append(jnp.transpose(np_p, STATE_TIME_MAJOR))
        conv_s.append(nc_s); pool_s.append(jnp.transpose(np_s, STATE_TIME_MAJOR))

    def stacked(parts):
        return parts[0] if depth == 1 else jnp.concatenate(parts, axis=0)

    return (hp, hs, stacked(conv_p), stacked(conv_s), stacked(pool_p), stacked(pool_s))
```
